```python
import math
import jax, jax.numpy as jnp
from jax import lax
import numpy as np

D_MODEL = 1024
BATCH = 8
SEQ = 2048
DEPTH = 1
DEC_BATCH = 128
DEC_SEQ = 4
PAST_LEN = 16384
PAGE_SIZE = 128

D_MIX = D_MODEL
D_POOL = D_MIX // 2
POOL_WINDOWS = (2, 4, 8, 16)
N_POOL_GROUPS = len(POOL_WINDOWS)
POOL_GROUP_DIM = D_POOL // N_POOL_GROUPS
POOL_BUF = max(POOL_WINDOWS) - 1
D_MLSTM = D_MIX - D_POOL
N_HEADS = 4
HEAD_DIM = D_MLSTM // N_HEADS
CHUNK = 64
D_IN = D_POOL + 4 * D_MLSTM + 2 * N_HEADS
SPLITS = (D_POOL, D_POOL + D_MLSTM, D_POOL + 2 * D_MLSTM, D_POOL + 3 * D_MLSTM,
          D_POOL + 4 * D_MLSTM, D_POOL + 4 * D_MLSTM + N_HEADS)
N_KEYS = 128
N_EXPERTS = N_KEYS * N_KEYS
PEER_HEADS = 8
D_KEY = 256
HALF_KEY = D_KEY // 2
TOPK = 16
TOKEN_BLOCK = 256
ALPHA = (2.0 * DEPTH) ** 0.25
BETA = (8.0 * DEPTH) ** -0.25
LN_EPS = 1e-5

kernel_name = 'hymba_pool_mlstm_peer_adaln_step'


def _ln(x, gain=None, bias=None):
    xf = x.astype(jnp.float32)
    mu = jnp.mean(xf, -1, keepdims=True)
    var = jnp.mean(jnp.square(xf - mu), -1, keepdims=True)
    y = (xf - mu) * lax.rsqrt(var + LN_EPS)
    if gain is not None:
        y = y * gain.astype(jnp.float32) + bias.astype(jnp.float32)
    return y.astype(x.dtype)


def _pool_mix(u, buf, pos0, w_pool, pool_scale):
    B, T, _ = u.shape
    z = jnp.concatenate([buf.astype(u.dtype), u], axis=1)
    cs = jnp.concatenate([jnp.zeros((B, 1, D_POOL), jnp.float32),
                          jnp.cumsum(z.astype(jnp.float32), axis=1)], axis=1)
    end = cs[:, POOL_BUF + 1:POOL_BUF + 1 + T]
    pos = pos0 + jnp.arange(T)
    means = []
    for g, w in enumerate(POOL_WINDOWS):
        sl = slice(g * POOL_GROUP_DIM, (g + 1) * POOL_GROUP_DIM)
        start = cs[:, POOL_BUF + 1 - w:POOL_BUF + 1 - w + T, sl]
        cnt = jnp.minimum(pos + 1, w).astype(jnp.float32)[None, :, None]
        means.append((end[:, :, sl] - start) / cnt)
    d = jnp.concatenate(means, -1) - u.astype(jnp.float32)
    d = d.reshape(B, T, N_POOL_GROUPS, POOL_GROUP_DIM).astype(u.dtype)
    y = jnp.einsum('btgc,gcd->btgd', d, w_pool).reshape(B, T, D_POOL)
    return y * pool_scale, z[:, -POOL_BUF:]


def _mlstm(q, k, v, ig, lf, S0, n0, m0):
    B, H, T, DH = q.shape
    L = CHUNK if T % CHUNK == 0 else T
    nc = T // L

    def split(a):
        return jnp.moveaxis(a.reshape(B, H, nc, L, *a.shape[3:]), 2, 0)

    causal = jnp.tril(jnp.ones((L, L), bool))

    def step(carry, inp):
        S, n, m = carry
        qc, kc, vc, ic, fc = inp
        b = jnp.cumsum(fc, axis=-1)
        a_prev = b + m[..., None]
        d = b[..., :, None] - b[..., None, :] + ic[..., None, :]
        d = jnp.where(causal, d, -jnp.inf)
        m_t = jnp.maximum(a_prev, jnp.max(d, -1))
        w_prev = jnp.exp(a_prev - m_t)
        qk = jnp.einsum('bhtd,bhsd->bhts', qc, kc) * jnp.exp(d - m_t[..., None])
        num = w_prev[..., None] * jnp.einsum('bhtd,bhde->bhte', qc, S) + jnp.einsum('bhts,bhse->bhte', qk, vc)
        den = w_prev * jnp.einsum('bhtd,bhd->bht', qc, n) + jnp.sum(qk, -1)
        h = num / jnp.maximum(jnp.abs(den), jnp.exp(-m_t))[..., None]
        m_new = m_t[..., -1]
        g_prev = jnp.exp(b[..., -1] + m - m_new)
        g_in = jnp.exp(b[..., -1:] - b + ic - m_new[..., None])
        kg = kc * g_in[..., None]
        S_new = g_prev[..., None, None] * S + jnp.einsum('bhsd,bhse->bhde', kg, vc)
        n_new = g_prev[..., None] * n + jnp.sum(kg, -2)
        return (S_new, n_new, m_new), h

    (S, n, m), hs = lax.scan(step, (S0, n0, m0), (split(q), split(k), split(v), split(ig), split(lf)))
    h = jnp.moveaxis(hs, 0, 2).reshape(B, H, T, DH)
    return h, S, n, m


def _mixer(u, pool_buf, S0, n0, m0, pos0, w_in, b_in, w_pool, pool_scale, mh_norm_g, w_out):
    B, T, _ = u.shape
    p = u @ w_in + b_in
    pu, q, k, v, o, gi, gf = jnp.split(p, SPLITS, axis=-1)
    y_pool, pool_new = _pool_mix(pu, pool_buf, pos0, w_pool, pool_scale)

    def heads(a):
        return a.reshape(B, T, N_HEADS, HEAD_DIM).transpose(0, 2, 1, 3).astype(jnp.float32)

    qh, kh, vh = heads(q), heads(k) * (HEAD_DIM ** -0.5), heads(v)
    ig = gi.astype(jnp.float32).transpose(0, 2, 1)
    lf = jax.nn.log_sigmoid(gf.astype(jnp.float32)).transpose(0, 2, 1)
    h, S, n, m = _mlstm(qh, kh, vh, ig, lf, S0.astype(jnp.float32), n0.astype(jnp.float32), m0.astype(jnp.float32))
    h = _ln(h).transpose(0, 2, 1, 3).reshape(B, T, D_MLSTM)
    y_ml = (h * mh_norm_g.astype(jnp.float32) * jax.nn.sigmoid(o.astype(jnp.float32))).astype(u.dtype)
    y = jnp.concatenate([y_pool, y_ml], -1) @ w_out
    return y, pool_new, S, n, m


def _peer(u, w_q, sub_keys, u_tab, v_tab):
    B, T, D = u.shape
    n = B * T
    blk = min(TOKEN_BLOCK, n)
    nb = -(-n // blk)
    xp = jnp.pad(u.reshape(n, D), ((0, nb * blk - n), (0, 0))).reshape(nb, blk, D)
    keys = sub_keys.astype(jnp.float32)

    def one(xb):
        q = (xb @ w_q).reshape(blk, PEER_HEADS, 2, HALF_KEY).astype(jnp.float32)
        s = jnp.einsum('nhpc,hpkc->nhpk', q, keys)
        sv, si = lax.top_k(s, TOPK)
        comb = (sv[:, :, 0, :, None] + sv[:, :, 1, None, :]).reshape(blk, PEER_HEADS, TOPK * TOPK)
        cidx = (si[:, :, 0, :, None] * N_KEYS + si[:, :, 1, None, :]).reshape(blk, PEER_HEADS, TOPK * TOPK)
        top, sel = lax.top_k(comb, TOPK)
        eidx = jnp.take_along_axis(cidx, sel, axis=-1)
        g = jax.nn.softmax(top, axis=-1)
        act = jax.nn.gelu(jnp.einsum('nd,nhkd->nhk', xb, u_tab[eidx]).astype(jnp.float32), approximate=False)
        coef = (g * act).astype(xb.dtype)
        return jnp.einsum('nhk,nhkd->nd', coef, v_tab[eidx])

    y = lax.map(one, xp).reshape(nb * blk, D)[:n]
    return y.reshape(B, T, D)


def _layer(x, c, pool_buf, S0, n0, m0, pos0, w_mod, b_mod, w_in, b_in, w_pool, pool_scale,
           mh_norm_g, w_out, ln1_g, ln1_b, w_q, sub_keys, u_tab, v_tab, ln2_g, ln2_b):
    mod = jax.nn.silu(c) @ w_mod + b_mod
    sh1, sc1, g1, sh2, sc2, g2 = [a[:, None, :] for a in jnp.split(mod, 6, axis=-1)]
    u1 = _ln(x) * (1 + sc1) + sh1
    y, pool_new, S, n, m = _mixer(u1, pool_buf, S0, n0, m0, pos0, w_in, b_in, w_pool, pool_scale, mh_norm_g, w_out)
    x = _ln(ALPHA * x + (1 + g1) * y, ln1_g, ln1_b)
    u2 = _ln(x) * (1 + sc2) + sh2
    y = _peer(u2, w_q, sub_keys, u_tab, v_tab)
    x = _ln(ALPHA * x + (1 + g2) * y, ln2_g, ln2_b)
    return x, pool_new, S, n, m


def setup_inputs(seed: int = 0) -> dict:
    key = jax.random.key(seed)
    ks = jax.random.split(key, 32)

    def nrm(k, shape, s):
        return jax.random.normal(k, shape, jnp.float32) * s

    L = DEPTH
    b_in = nrm(ks[12], (L, D_IN), 0.02)
    b_in = b_in.at[:, D_IN - N_HEADS:].add(jnp.linspace(3.0, 6.0, N_HEADS))
    return {
        'x_prompt': nrm(ks[0], (BATCH, SEQ, D_MODEL), 1.0),
        'x_sample': nrm(ks[1], (DEC_BATCH, DEC_SEQ, D_MODEL), 1.0),
        'c_prompt': nrm(ks[2], (BATCH, D_MODEL), 1.0),
        'c_sample': nrm(ks[3], (DEC_BATCH, D_MODEL), 1.0),
        'state_pool': nrm(ks[4], (L, DEC_BATCH, POOL_BUF, D_POOL), 1.0),
        'state_C': nrm(ks[5], (L, DEC_BATCH, N_HEADS, HEAD_DIM, HEAD_DIM), 0.1),
        'state_n': nrm(ks[6], (L, DEC_BATCH, N_HEADS, HEAD_DIM), 0.3),
        'state_m': nrm(ks[7], (L, DEC_BATCH, N_HEADS), 1.0),
        'w_mod': nrm(ks[8], (L, D_MODEL, 6 * D_MODEL), 0.1 * D_MODEL ** -0.5),
        'b_mod': nrm(ks[9], (L, 6 * D_MODEL), 0.02),
        'w_in': nrm(ks[10], (L, D_MODEL, D_IN), D_MODEL ** -0.5),
        'b_in': b_in,
        'w_pool': nrm(ks[13], (L, N_POOL_GROUPS, POOL_GROUP_DIM, POOL_GROUP_DIM), POOL_GROUP_DIM ** -0.5),
        'pool_scale': 1.0 + nrm(ks[14], (L, D_POOL), 0.1),
        'mh_norm_g': 1.0 + nrm(ks[15], (L, D_MLSTM), 0.1),
        'w_out': nrm(ks[16], (L, D_MIX, D_MODEL), BETA * D_MIX ** -0.5),
        'ln1_g': 1.0 + nrm(ks[17], (L, D_MODEL), 0.1),
        'ln1_b': nrm(ks[18], (L, D_MODEL), 0.02),
        'w_q': nrm(ks[19], (L, D_MODEL, PEER_HEADS * D_KEY), D_MODEL ** -0.5),
        'sub_keys': nrm(ks[20], (L, PEER_HEADS, 2, N_KEYS, HALF_KEY), HALF_KEY ** -0.5),
        'u_tab': nrm(ks[21], (L, N_EXPERTS, D_MODEL), D_MODEL ** -0.5),
        'v_tab': nrm(ks[22], (L, N_EXPERTS, D_MODEL), BETA * PEER_HEADS ** -0.5),
        'ln2_g': 1.0 + nrm(ks[23], (L, D_MODEL), 0.1),
        'ln2_b': nrm(ks[24], (L, D_MODEL), 0.02),
    }


def reference(x_prompt, x_sample, c_prompt, c_sample, state_pool, state_C, state_n, state_m,
              w_mod, b_mod, w_in, b_in, w_pool, pool_scale, mh_norm_g, w_out, ln1_g, ln1_b,
              w_q, sub_keys, u_tab, v_tab, ln2_g, ln2_b):
    B = x_prompt.shape[0]
    yp, ys = x_prompt, x_sample
    pp, Cp, np_, mp, ps, Cs, ns, ms = [], [], [], [], [], [], [], []
    for l in range(DEPTH):
        wl = (w_mod[l], b_mod[l], w_in[l], b_in[l], w_pool[l], pool_scale[l], mh_norm_g[l], w_out[l],
              ln1_g[l], ln1_b[l], w_q[l], sub_keys[l], u_tab[l], v_tab[l], ln2_g[l], ln2_b[l])
        yp, a, b, c, d = _layer(yp, c_prompt,
                                jnp.zeros((B, POOL_BUF, D_POOL), x_prompt.dtype),
                                jnp.zeros((B, N_HEADS, HEAD_DIM, HEAD_DIM), jnp.float32),
                                jnp.zeros((B, N_HEADS, HEAD_DIM), jnp.float32),
                                jnp.zeros((B, N_HEADS), jnp.float32), 0, *wl)
        pp.append(a); Cp.append(b); np_.append(c); mp.append(d)
        ys, a, b, c, d = _layer(ys, c_sample, state_pool[l], state_C[l], state_n[l], state_m[l], PAST_LEN, *wl)
        ps.append(a); Cs.append(b); ns.append(c); ms.append(d)
    sd = state_C.dtype
    pool_p = jnp.stack(pp).astype(state_pool.dtype)
    C_p = jnp.stack(Cp).astype(sd)
    n_p = jnp.stack(np_).astype(state_n.dtype)
    m_p = jnp.stack(mp).astype(state_m.dtype)
    pool_s = jnp.stack(ps).astype(state_pool.dtype)
    C_s = jnp.stack(Cs).astype(sd)
    n_s = jnp.stack(ns).astype(state_n.dtype)
    m_s = jnp.stack(ms).astype(state_m.dtype)
    return (yp, ys, pool_p, C_p, n_p, m_p, pool_s, C_s, n_s, m_s)
```

```python
import functools
import math

import numpy as np
import jax
import jax.numpy as jnp
from jax import lax
from jax.experimental import pallas as pl
from jax.experimental.pallas import tpu as pltpu

D_MODEL = 1024
DEPTH = 1
PAST_LEN = 16384
D_POOL = 512
POOL_WINDOWS = (2, 4, 8, 16)
POOL_GROUP_DIM = 128
POOL_BUF = 15
D_MLSTM = 512
N_HEADS = 4
HEAD_DIM = 128
D_IN = D_POOL + 4 * D_MLSTM + 2 * N_HEADS
D_MAIN = D_POOL + 4 * D_MLSTM
D_IN_PAD = D_MAIN + 128
N_KEYS = 128
N_EXPERTS = N_KEYS * N_KEYS
PEER_HEADS = 8
HALF_KEY = 128
TOPK = 16
ALPHA = (2.0 * DEPTH) ** 0.25
LN_EPS = 1e-5
NEG = -1e30

LANES = 128
VMEM_LIMIT = 56 * 1024 * 1024

F32 = jnp.float32
BF16 = jnp.bfloat16


def _ln_rows(x):
    mu = jnp.mean(x, axis=-1, keepdims=True)
    xc = x - mu
    var = jnp.mean(xc * xc, axis=-1, keepdims=True)
    return xc * lax.rsqrt(var + LN_EPS)


def _params(sem):
    return pltpu.CompilerParams(dimension_semantics=sem, vmem_limit_bytes=VMEM_LIMIT)


def _mod_body(c_ref, w_ref, b_ref, o_ref):
    c = c_ref[...]
    a = (c * jax.nn.sigmoid(c)).astype(BF16)
    o_ref[...] = jnp.dot(a, w_ref[...].astype(BF16), preferred_element_type=F32) + b_ref[...]


def _mod_call(c, w_mod, b_mod):
    nb = c.shape[0]
    tn = 1536
    return pl.pallas_call(
        _mod_body,
        grid=(6 * D_MODEL // tn,),
        in_specs=[pl.BlockSpec((nb, D_MODEL), lambda j: (0, 0)),
                  pl.BlockSpec((D_MODEL, tn), lambda j: (0, j)),
                  pl.BlockSpec((1, tn), lambda j: (0, j))],
        out_specs=pl.BlockSpec((nb, tn), lambda j: (0, j)),
        out_shape=jax.ShapeDtypeStruct((nb, 6 * D_MODEL), F32),
        compiler_params=_params(("arbitrary",)),
        name="mod",
    )(c, w_mod, b_mod.reshape(1, -1))


def _inproj_body(x_ref, sh_ref, sc_ref, w_ref, b_ref, p_ref):
    u = _ln_rows(x_ref[...]) * (1.0 + sc_ref[0]) + sh_ref[0]
    p_ref[...] = jnp.dot(u.astype(BF16), w_ref[...], preferred_element_type=F32) + b_ref[...]


def _mod_spec(mr, tiles_per_mod):
    return pl.BlockSpec((1, mr, D_MODEL), lambda i: (i // tiles_per_mod, 0, 0))


def _inproj_call(x2d, sh, sc, w_in_p, b_in_p, tm, tiles_per_mod):
    n = x2d.shape[0]
    mr = sh.shape[1]
    return pl.pallas_call(
        _inproj_body,
        grid=(n // tm,),
        in_specs=[pl.BlockSpec((tm, D_MODEL), lambda i: (i, 0)),
                  _mod_spec(mr, tiles_per_mod), _mod_spec(mr, tiles_per_mod),
                  pl.BlockSpec((D_MODEL, D_IN_PAD), lambda i: (0, 0)),
                  pl.BlockSpec((1, D_IN_PAD), lambda i: (0, 0))],
        out_specs=pl.BlockSpec((tm, D_IN_PAD), lambda i: (i, 0)),
        out_shape=jax.ShapeDtypeStruct((n, D_IN_PAD), F32),
        compiler_params=_params(("parallel",)),
        name="inproj",
    )(x2d, sh, sc, w_in_p, b_in_p)


def _pool_body(*refs, tt, pos0, t_last, has_state):
    if has_state:
        pu_ref, hist_ref, w_ref, scale_ref, y_ref, new_ref, z_ref = refs
    else:
        pu_ref, w_ref, scale_ref, y_ref, new_ref, z_ref = refs
    j = pl.program_id(1)
    hb = POOL_BUF + 1

    @pl.when(j == 0)
    def _():
        z_ref[0:hb, :] = jnp.zeros((hb, D_POOL), F32)
        if has_state:
            z_ref[1:hb, :] = hist_ref[...]

    @pl.when(j > 0)
    def _():
        z_ref[0:hb, :] = z_ref[tt:tt + hb, :]

    tile = pu_ref[...]
    z_ref[hb:hb + tt, :] = tile
    pos = pos0 + j * tt + lax.broadcasted_iota(jnp.int32, (tt, 1), 0)
    for g, w in enumerate(POOL_WINDOWS):
        sl = slice(g * POOL_GROUP_DIM, (g + 1) * POOL_GROUP_DIM)
        acc = tile[:, sl]
        for k in range(1, w):
            acc = acc + z_ref[hb - k:hb - k + tt, sl]
        cnt = jnp.minimum(pos + 1, w).astype(F32)
        d = acc / cnt - tile[:, sl]
        y = jnp.dot(d.astype(BF16), w_ref[g], preferred_element_type=F32)
        y_ref[:, sl] = y * scale_ref[:, sl]

    @pl.when(j == pl.num_programs(1) - 1)
    def _():
        new_ref[...] = z_ref[hb + t_last - POOL_BUF:hb + t_last, :]


def _pool_call(p3, hist, w_pool_b, pool_scale, tt, pos0, t_valid):
    b, tp, _ = p3.shape
    nt = tp // tt
    t_last = t_valid - (nt - 1) * tt
    has_state = hist is not None
    in_specs = [pl.BlockSpec((None, tt, D_POOL), lambda i, j: (i, j, 0))]
    args = [p3]
    if has_state:
        in_specs.append(pl.BlockSpec((None, POOL_BUF, D_POOL), lambda i, j: (i, 0, 0)))
        args.append(hist)
    in_specs += [pl.BlockSpec((4, POOL_GROUP_DIM, POOL_GROUP_DIM), lambda i, j: (0, 0, 0)),
                 pl.BlockSpec((1, D_POOL), lambda i, j: (0, 0))]
    args += [w_pool_b, pool_scale]
    return pl.pallas_call(
        functools.partial(_pool_body, tt=tt, pos0=pos0, t_last=t_last, has_state=has_state),
        grid=(b, nt),
        in_specs=in_specs,
        out_specs=[pl.BlockSpec((None, tt, D_POOL), lambda i, j: (i, j, 0)),
                   pl.BlockSpec((None, POOL_BUF, D_POOL), lambda i, j: (i, 0, 0))],
        out_shape=[jax.ShapeDtypeStruct((b, tp, D_POOL), F32),
                   jax.ShapeDtypeStruct((b, POOL_BUF, D_POOL), F32)],
        scratch_shapes=[pltpu.VMEM((POOL_BUF + 1 + tt, D_POOL), F32)],
        compiler_params=_params(("parallel", "arbitrary")),
        name="pool",
    )(*args)


def _log_sigmoid(x):
    return jnp.minimum(x, 0.0) - jnp.log1p(jnp.exp(-jnp.abs(x)))


def _mlstm_body(*refs, L, t_valid, has_state):
    if has_state:
        q_ref, k_ref, v_ref, g_ref, s0_ref, n0_ref, m0_ref, h_ref, s_ref, n_ref, m_ref = refs
    else:
        q_ref, k_ref, v_ref, g_ref, h_ref, s_ref, n_ref, m_ref = refs
    c = pl.program_id(1)

    @pl.when(c == 0)
    def _():
        if has_state:
            s_ref[...] = s0_ref[...]
            n_ref[...] = n0_ref[...]
            m_ref[...] = m0_ref[...]
        else:
            s_ref[...] = jnp.zeros(s_ref.shape, F32)
            n_ref[...] = jnp.zeros(n_ref.shape, F32)
            m_ref[...] = jnp.zeros(m_ref.shape, F32)

    g = g_ref[...]
    row = c * L + lax.broadcasted_iota(jnp.int32, (L, 1), 0)
    valid = row < t_valid
    lf_all = jnp.where(valid, _log_sigmoid(g), 0.0)
    ri = lax.broadcasted_iota(jnp.int32, (L, L), 0)
    ci = lax.broadcasted_iota(jnp.int32, (L, L), 1)
    causal = ri >= ci
    eye = ri == ci
    b_all = jnp.dot(causal.astype(F32), lf_all, preferred_element_type=F32,
                    precision=lax.Precision.HIGHEST)
    q = q_ref[...]
    k = k_ref[...]
    v = v_ref[...]
    for hd in range(N_HEADS):
        sl = slice(hd * HEAD_DIM, (hd + 1) * HEAD_DIM)
        ig_col = jnp.where(valid, g[:, hd:hd + 1], NEG)
        b_col = b_all[:, N_HEADS + hd:N_HEADS + hd + 1]
        c_row = jnp.sum(jnp.where(eye, ig_col - b_col, 0.0), axis=0, keepdims=True)
        d = jnp.where(causal, b_col + c_row, NEG)
        m_prev = m_ref[hd:hd + 1, 0:1]
        a_prev = b_col + m_prev
        m_t = jnp.maximum(a_prev, jnp.max(d, axis=1, keepdims=True))
        w_prev = jnp.exp(a_prev - m_t)
        pw = jnp.exp(d - m_t)
        qh = q[:, sl]
        kh = jnp.where(valid, k[:, sl] * (HEAD_DIM ** -0.5), 0.0)
        vh = jnp.where(valid, v[:, sl], 0.0)
        qb = qh.astype(BF16)
        kb = kh.astype(BF16)
        vb = vh.astype(BF16)
        qk = lax.dot_general(qb, kb, (((1,), (1,)), ((), ())), preferred_element_type=F32) * pw
        s_old = s_ref[hd]
        n_old = n_ref[hd:hd + 1, :]
        num = w_prev * jnp.dot(qb, s_old.astype(BF16), preferred_element_type=F32) \
            + jnp.dot(qk.astype(BF16), vb, preferred_element_type=F32)
        den = w_prev * jnp.sum(qh * n_old, axis=1, keepdims=True) + jnp.sum(qk, axis=1, keepdims=True)
        h_ref[:, sl] = num / jnp.maximum(jnp.abs(den), jnp.exp(-m_t))
        m_new = m_t[L - 1:L, :]
        b_last = b_col[L - 1:L, :]
        g_prev = jnp.exp(b_last + m_prev - m_new)
        g_in = jnp.exp(b_last - b_col + ig_col - m_new)
        kg = kh * g_in
        s_ref[hd] = g_prev * s_old + lax.dot_general(kg.astype(BF16), vb, (((0,), (0,)), ((), ())),
                                                     preferred_element_type=F32)
        n_ref[hd:hd + 1, :] = g_prev * n_old + jnp.sum(kg, axis=0, keepdims=True)
        m_ref[hd:hd + 1, :] = jnp.broadcast_to(m_new, (1, LANES))


def _mlstm_call(p3, state, L, t_valid):
    b, tp, _ = p3.shape
    nc = tp // L
    has_state = state is not None
    col = lambda blk: pl.BlockSpec((None, L, D_MLSTM), lambda i, j, blk=blk: (i, j, blk))
    in_specs = [col(1), col(2), col(3),
                pl.BlockSpec((None, L, LANES), lambda i, j: (i, j, D_MAIN // LANES))]
    args = [p3, p3, p3, p3]
    st_specs = [pl.BlockSpec((None, N_HEADS, HEAD_DIM, HEAD_DIM), lambda i, j: (i, 0, 0, 0)),
                pl.BlockSpec((None, N_HEADS, HEAD_DIM), lambda i, j: (i, 0, 0)),
                pl.BlockSpec((None, N_HEADS, LANES), lambda i, j: (i, 0, 0))]
    if has_state:
        in_specs += st_specs
        args += list(state)
    return pl.pallas_call(
        functools.partial(_mlstm_body, L=L, t_valid=t_valid, has_state=has_state),
        grid=(b, nc),
        in_specs=in_specs,
        out_specs=[pl.BlockSpec((None, L, D_MLSTM), lambda i, j: (i, j, 0))] + st_specs,
        out_shape=[jax.ShapeDtypeStruct((b, tp, D_MLSTM), F32),
                   jax.ShapeDtypeStruct((b, N_HEADS, HEAD_DIM, HEAD_DIM), F32),
                   jax.ShapeDtypeStruct((b, N_HEADS, HEAD_DIM), F32),
                   jax.ShapeDtypeStruct((b, N_HEADS, LANES), F32)],
        compiler_params=_params(("parallel", "arbitrary")),
        name="mlstm",
    )(*args)


def _mixout_body(yp_ref, h_ref, o_ref, x_ref, g1_ref, sh2_ref, sc2_ref, mg_ref, w_ref, lg_ref, lb_ref,
                 x1_ref, u2_ref):
    h = h_ref[...]
    o = o_ref[...]
    parts = [yp_ref[...].astype(BF16)]
    for hd in range(N_HEADS):
        sl = slice(hd * HEAD_DIM, (hd + 1) * HEAD_DIM)
        yh = _ln_rows(h[:, sl]) * mg_ref[:, sl] * jax.nn.sigmoid(o[:, sl])
        parts.append(yh.astype(BF16))
    ycat = jnp.concatenate(parts, axis=1)
    y = jnp.dot(ycat, w_ref[...], preferred_element_type=F32)
    x1 = _ln_rows(ALPHA * x_ref[...] + (1.0 + g1_ref[0]) * y) * lg_ref[...] + lb_ref[...]
    x1_ref[...] = x1
    u2_ref[...] = (_ln_rows(x1) * (1.0 + sc2_ref[0]) + sh2_ref[0]).astype(BF16)


def _mixout_call(ypool, h, p, x2d, g1, sh2, sc2, mh_g, w_out_b, ln1_g, ln1_b, tm, tiles_per_mod):
    n = x2d.shape[0]
    mr = g1.shape[1]
    row = lambda w, blk=0: pl.BlockSpec((tm, w), lambda i, blk=blk: (i, blk))
    full = lambda r, c: pl.BlockSpec((r, c), lambda i: (0, 0))
    ms = _mod_spec(mr, tiles_per_mod)
    return pl.pallas_call(
        _mixout_body,
        grid=(n // tm,),
        in_specs=[row(D_POOL), row(D_MLSTM), row(D_MLSTM, 4), row(D_MODEL), ms, ms, ms,
                  full(1, D_MLSTM), full(D_MODEL, D_MODEL), full(1, D_MODEL), full(1, D_MODEL)],
        out_specs=[row(D_MODEL), row(D_MODEL)],
        out_shape=[jax.ShapeDtypeStruct((n, D_MODEL), F32), jax.ShapeDtypeStruct((n, D_MODEL), BF16)],
        compiler_params=_params(("parallel",)),
        name="mixout",
    )(ypool, h, p, x2d, g1, sh2, sc2, mh_g, w_out_b, ln1_g, ln1_b)


N_RANK = TOPK + 1
_STAIRS = [(a, b) for a in range(N_RANK) for b in range(N_RANK) if (a + 1) * (b + 1) <= N_RANK]
N_CAND = len(_STAIRS)
N_CAND_PAD = -(-N_CAND // 8) * 8
N_RANK_PAD = -(-N_RANK // 8) * 8


def _top_values(work, n, store):
    for r in range(n):
        m = jnp.max(work, axis=0, keepdims=True)
        store(r, m)
        work = jnp.where(work == m, -jnp.inf, work)


def _route_body(u_ref, wq_ref, keys_ref, t_ref, a_ref, s1_ref, e1_ref, s_scr, sv_scr, cand_scr, ecand_scr, *, tb):
    q = jnp.dot(u_ref[...], wq_ref[...], preferred_element_type=F32)
    for hp in range(2 * PEER_HEADS):
        qhp = q[:, hp * HALF_KEY:(hp + 1) * HALF_KEY].astype(BF16)
        s_scr[hp] = lax.dot_general(keys_ref[hp], qhp, (((1,), (1,)), ((), ())),
                                    preferred_element_type=F32)

    def rank_body(hp, carry):
        def store(r, m):
            sv_scr[hp, r:r + 1, :] = m
        _top_values(s_scr[hp], N_RANK, store)
        return carry

    lax.fori_loop(0, 2 * PEER_HEADS, rank_body, 0)

    if N_CAND_PAD > N_CAND:
        cand_scr[N_CAND:N_CAND_PAD, :] = jnp.full((N_CAND_PAD - N_CAND, tb), -jnp.inf, F32)
        ecand_scr[N_CAND:N_CAND_PAD, :] = jnp.zeros((N_CAND_PAD - N_CAND, tb), F32)
    for h in range(PEER_HEADS):
        sv0 = sv_scr[2 * h]
        sv1 = sv_scr[2 * h + 1]
        ev0 = jnp.exp(sv0 - sv0[0:1])
        ev1 = jnp.exp(sv1 - sv1[0:1])
        for idx, (a, b) in enumerate(_STAIRS):
            cand_scr[idx:idx + 1, :] = sv0[a:a + 1] + sv1[b:b + 1]
            ecand_scr[idx:idx + 1, :] = ev0[a:a + 1] * ev1[b:b + 1]
        cand = cand_scr[...]
        tops = {}

        def keep(r, m):
            tops[r] = m
        _top_values(cand, N_RANK, keep)
        tau = 0.5 * (tops[TOPK - 1] + tops[TOPK])
        z = jnp.sum(jnp.where(cand > tau, ecand_scr[...], 0.0), axis=0, keepdims=True)
        s0 = s_scr[2 * h]
        s1 = s_scr[2 * h + 1]
        tmat = tau - s0
        amat = jnp.exp(s0 - sv0[0:1]) / z
        emat = jnp.exp(s1 - sv1[0:1])
        rows = slice(h * N_KEYS, (h + 1) * N_KEYS)
        for l in range(tb // LANES):
            ls = slice(l * LANES, (l + 1) * LANES)
            t_ref[l, rows, :] = tmat[:, ls]
            a_ref[l, rows, :] = amat[:, ls]
            s1_ref[l, rows, :] = s1[:, ls]
            e1_ref[l, rows, :] = emat[:, ls]


def _route_call(u2, wq_b, keys_b, tb):
    n = u2.shape[0]
    hk = PEER_HEADS * N_KEYS
    out_spec = pl.BlockSpec((tb // LANES, hk, LANES), lambda i: (i, 0, 0))
    out_sds = jax.ShapeDtypeStruct((n // LANES, hk, LANES), F32)
    return pl.pallas_call(
        functools.partial(_route_body, tb=tb),
        grid=(n // tb,),
        in_specs=[pl.BlockSpec((tb, D_MODEL), lambda i: (i, 0)),
                  pl.BlockSpec((D_MODEL, 2 * hk), lambda i: (0, 0)),
                  pl.BlockSpec((2 * PEER_HEADS, N_KEYS, HALF_KEY), lambda i: (0, 0, 0))],
        out_specs=[out_spec] * 4,
        out_shape=[out_sds] * 4,
        scratch_shapes=[pltpu.VMEM((2 * PEER_HEADS, N_KEYS, tb), F32),
                        pltpu.VMEM((2 * PEER_HEADS, N_RANK_PAD, tb), F32),
                        pltpu.VMEM((N_CAND_PAD, tb), F32),
                        pltpu.VMEM((N_CAND_PAD, tb), F32)],
        compiler_params=_params(("parallel",)),
        name="route",
    )(u2, wq_b, keys_b)


SUB = 16


def _gelu(x):
    return 0.5 * x * (1.0 + lax.erf(x * np.float32(math.sqrt(0.5))))


def _peer_body(u_ref, t_ref, a_ref, s1_ref, e1_ref, ut_ref, vt_ref, x1_ref, g2_ref, lg_ref, lb_ref, o_ref,
               uT_scr, y_scr, act_scr, coef_scr, *, tb, ek):
    c = pl.program_id(1)
    ni = ek // N_KEYS

    @pl.when(c == 0)
    def _():
        uT_scr[...] = u_ref[...].astype(F32).T.astype(BF16)
        y_scr[...] = jnp.zeros(y_scr.shape, F32)

    act_scr[...] = jnp.dot(ut_ref[...], uT_scr[...], preferred_element_type=F32)

    def row_body(ii, carry):
        i = c * ni + ii
        base = pl.multiple_of(ii * N_KEYS, N_KEYS)
        for l in range(tb // LANES):
            ls = slice(l * LANES, (l + 1) * LANES)
            tb_h = []
            ab_h = []
            for h in range(PEER_HEADS):
                tb_h.append(jnp.broadcast_to(t_ref[l, pl.ds(h * N_KEYS + i, 1), :], (SUB, LANES)))
                ab_h.append(jnp.broadcast_to(a_ref[l, pl.ds(h * N_KEYS + i, 1), :], (SUB, LANES)))
            for jg in range(N_KEYS // SUB):
                acc = jnp.zeros((SUB, LANES), F32)
                for h in range(PEER_HEADS):
                    rs = slice(h * N_KEYS + jg * SUB, h * N_KEYS + (jg + 1) * SUB)
                    acc = acc + jnp.where(s1_ref[l, rs, :] >= tb_h[h], e1_ref[l, rs, :], 0.0) * ab_h[h]
                rows = pl.ds(base + jg * SUB, SUB)
                coef_scr[rows, ls] = (acc * _gelu(act_scr[rows, ls])).astype(BF16)
        return carry

    lax.fori_loop(0, ni, row_body, 0)
    y_scr[...] += jnp.dot(vt_ref[...], coef_scr[...], preferred_element_type=F32)

    @pl.when(c == pl.num_programs(1) - 1)
    def _():
        y = y_scr[...].T
        z = ALPHA * x1_ref[...] + (1.0 + g2_ref[0]) * y
        o_ref[...] = _ln_rows(z) * lg_ref[...] + lb_ref[...]


def _peer_call(u2, tT, aT, s1T, e1T, u_b, vt_b, x1, g2, ln2_g, ln2_b, tb, ek, tiles_per_mod):
    n = u2.shape[0]
    mr = g2.shape[1]
    hk = PEER_HEADS * N_KEYS
    rt = pl.BlockSpec((tb // LANES, hk, LANES), lambda i, c: (i, 0, 0))
    return pl.pallas_call(
        functools.partial(_peer_body, tb=tb, ek=ek),
        grid=(n // tb, N_EXPERTS // ek),
        in_specs=[pl.BlockSpec((tb, D_MODEL), lambda i, c: (i, 0)),
                  rt, rt, rt, rt,
                  pl.BlockSpec((ek, D_MODEL), lambda i, c: (c, 0)),
                  pl.BlockSpec((D_MODEL, ek), lambda i, c: (0, c)),
                  pl.BlockSpec((tb, D_MODEL), lambda i, c: (i, 0)),
                  pl.BlockSpec((1, mr, D_MODEL), lambda i, c: (i // tiles_per_mod, 0, 0)),
                  pl.BlockSpec((1, D_MODEL), lambda i, c: (0, 0)),
                  pl.BlockSpec((1, D_MODEL), lambda i, c: (0, 0))],
        out_specs=pl.BlockSpec((tb, D_MODEL), lambda i, c: (i, 0)),
        out_shape=jax.ShapeDtypeStruct((n, D_MODEL), F32),
        scratch_shapes=[pltpu.VMEM((D_MODEL, tb), BF16),
                        pltpu.VMEM((D_MODEL, tb), F32),
                        pltpu.VMEM((ek, tb), F32),
                        pltpu.VMEM((ek, tb), BF16)],
        compiler_params=_params(("parallel", "arbitrary")),
        name="peer",
    )(u2, tT, aT, s1T, e1T, u_b, vt_b, x1, g2, ln2_g, ln2_b)


def _layer(x, mod, pool_buf, state, pos0, wts, *, tm, tt, L, tb_route, tb_peer, ek):
    (w_in_p, b_in_p, w_pool_b, pool_scale, mh_g, w_out_b, ln1_g, ln1_b, wq_b, keys_b, u_b, vt_b,
     ln2_g, ln2_b) = wts
    b, t, _ = x.shape
    n = b * t
    x2d = x.reshape(n, D_MODEL)
    if t >= tm:
        mod3 = mod.reshape(b, 1, 6 * D_MODEL)
        tpm = lambda tile: t // tile
    else:
        assert n == tm
        mod3 = jnp.repeat(mod, t, axis=0).reshape(1, n, 6 * D_MODEL)
        tpm = lambda tile: 1
    sh1, sc1, g1, sh2, sc2, g2 = [mod3[:, :, k * D_MODEL:(k + 1) * D_MODEL] for k in range(6)]

    p = _inproj_call(x2d, sh1, sc1, w_in_p, b_in_p, tm, tpm(tm))
    p3 = p.reshape(b, t, D_IN_PAD)
    tp = -(-t // L) * L
    if tp != t:
        p3 = jnp.pad(p3, ((0, 0), (0, tp - t), (0, 0)))
    ypool, pool_new = _pool_call(p3, pool_buf, w_pool_b, pool_scale, min(tt, tp), pos0, t)
    h, s_new, n_new, m_new = _mlstm_call(p3, state, L, t)
    if tp != t:
        ypool = ypool[:, :t]
        h = h[:, :t]
    x1, u2 = _mixout_call(ypool.reshape(n, D_POOL), h.reshape(n, D_MLSTM), p, x2d, g1, sh2, sc2,
                          mh_g, w_out_b, ln1_g, ln1_b, tm, tpm(tm))
    tT, aT, s1T, e1T = _route_call(u2, wq_b, keys_b, tb_route)
    y = _peer_call(u2, tT, aT, s1T, e1T, u_b, vt_b, x1, g2, ln2_g, ln2_b, tb_peer, ek, tpm(tb_peer))
    return y.reshape(b, t, D_MODEL), pool_new, s_new, n_new, m_new[:, :, 0]


def kernel(x_prompt, x_sample, c_prompt, c_sample, state_pool, state_C, state_n, state_m, w_mod, b_mod, w_in,
           b_in, w_pool, pool_scale, mh_norm_g, w_out, ln1_g, ln1_b, w_q, sub_keys, u_tab, v_tab, ln2_g, ln2_b):
    assert w_mod.shape[0] == DEPTH == 1
    bp = x_prompt.shape[0]
    w_in_p = jnp.pad(w_in[0], ((0, 0), (0, D_IN_PAD - D_IN))).astype(BF16)
    b_in_p = jnp.pad(b_in[0], (0, D_IN_PAD - D_IN)).reshape(1, D_IN_PAD)
    wts = (w_in_p, b_in_p, w_pool[0].astype(BF16), pool_scale[0].reshape(1, D_POOL),
           mh_norm_g[0].reshape(1, D_MLSTM), w_out[0].astype(BF16), ln1_g[0].reshape(1, D_MODEL),
           ln1_b[0].reshape(1, D_MODEL), w_q[0].astype(BF16),
           sub_keys[0].reshape(2 * PEER_HEADS, N_KEYS, HALF_KEY).astype(BF16),
           u_tab[0].astype(BF16), v_tab[0].astype(BF16).T,
           ln2_g[0].reshape(1, D_MODEL), ln2_b[0].reshape(1, D_MODEL))
    mod = _mod_call(jnp.concatenate([c_prompt, c_sample], axis=0), w_mod[0], b_mod[0])

    yp, pool_p, c_p, n_p, m_p = _layer(x_prompt, mod[:bp], None, None, 0, wts,
                                       tm=512, tt=256, L=128, tb_route=256, tb_peer=512, ek=1024)
    state = (state_C[0], state_n[0],
             jnp.broadcast_to(state_m[0][:, :, None], state_m.shape[1:] + (LANES,)))
    ys, pool_s, c_s, n_s, m_s = _layer(x_sample, mod[bp:], state_pool[0], state, PAST_LEN, wts,
                                       tm=512, tt=8, L=8, tb_route=256, tb_peer=512, ek=1024)
    return (yp, ys, pool_p[None], c_p[None], n_p[None], m_p[None],
            pool_s[None], c_s[None], n_s[None], m_s[None])
```

```python
import functools
import math

import numpy as np
import jax
import jax.numpy as jnp
from jax import lax
from jax.experimental import pallas as pl
from jax.experimental.pallas import tpu as pltpu

D_MODEL = 1024
DEPTH = 1
PAST_LEN = 16384
D_POOL = 512
POOL_WINDOWS = (2, 4, 8, 16)
POOL_GROUP_DIM = 128
POOL_BUF = 15
D_MLSTM = 512
N_HEADS = 4
HEAD_DIM = 128
D_IN = D_POOL + 4 * D_MLSTM + 2 * N_HEADS
D_MAIN = D_POOL + 4 * D_MLSTM
D_IN_PAD = D_MAIN + 128
N_KEYS = 128
N_EXPERTS = N_KEYS * N_KEYS
PEER_HEADS = 8
HALF_KEY = 128
TOPK = 16
ALPHA = (2.0 * DEPTH) ** 0.25
LN_EPS = 1e-5
NEG = -1e30

LANES = 128
VMEM_LIMIT = 56 * 1024 * 1024

F32 = jnp.float32
BF16 = jnp.bfloat16


def _ln_rows(x):
    mu = jnp.mean(x, axis=-1, keepdims=True)
    xc = x - mu
    var = jnp.mean(xc * xc, axis=-1, keepdims=True)
    return xc * lax.rsqrt(var + LN_EPS)


def _params(sem):
    return pltpu.CompilerParams(dimension_semantics=sem, vmem_limit_bytes=VMEM_LIMIT)


def _mod_body(c_ref, w_ref, b_ref, o_ref):
    c = c_ref[...]
    a = (c * jax.nn.sigmoid(c)).astype(BF16)
    o_ref[...] = jnp.dot(a, w_ref[...].astype(BF16), preferred_element_type=F32) + b_ref[...]


def _mod_call(c, w_mod, b_mod):
    nb = c.shape[0]
    tn = 1536
    return pl.pallas_call(
        _mod_body,
        grid=(6 * D_MODEL // tn,),
        in_specs=[pl.BlockSpec((nb, D_MODEL), lambda j: (0, 0)),
                  pl.BlockSpec((D_MODEL, tn), lambda j: (0, j)),
                  pl.BlockSpec((1, tn), lambda j: (0, j))],
        out_specs=pl.BlockSpec((nb, tn), lambda j: (0, j)),
        out_shape=jax.ShapeDtypeStruct((nb, 6 * D_MODEL), F32),
        compiler_params=_params(("arbitrary",)),
        name="mod",
    )(c, w_mod, b_mod.reshape(1, -1))


def _inproj_body(x_ref, sh_ref, sc_ref, w_ref, b_ref, p_ref):
    u = _ln_rows(x_ref[...]) * (1.0 + sc_ref[0]) + sh_ref[0]
    p_ref[...] = jnp.dot(u.astype(BF16), w_ref[...], preferred_element_type=F32) + b_ref[...]


def _mod_spec(mr, tiles_per_mod):
    return pl.BlockSpec((1, mr, D_MODEL), lambda i: (i // tiles_per_mod, 0, 0))


def _inproj_call(x2d, sh, sc, w_in_p, b_in_p, tm, tiles_per_mod):
    n = x2d.shape[0]
    mr = sh.shape[1]
    return pl.pallas_call(
        _inproj_body,
        grid=(n // tm,),
        in_specs=[pl.BlockSpec((tm, D_MODEL), lambda i: (i, 0)),
                  _mod_spec(mr, tiles_per_mod), _mod_spec(mr, tiles_per_mod),
                  pl.BlockSpec((D_MODEL, D_IN_PAD), lambda i: (0, 0)),
                  pl.BlockSpec((1, D_IN_PAD), lambda i: (0, 0))],
        out_specs=pl.BlockSpec((tm, D_IN_PAD), lambda i: (i, 0)),
        out_shape=jax.ShapeDtypeStruct((n, D_IN_PAD), F32),
        compiler_params=_params(("parallel",)),
        name="inproj",
    )(x2d, sh, sc, w_in_p, b_in_p)


def _pool_body(*refs, tt, pos0, t_last, has_state):
    if has_state:
        pu_ref, hist_ref, w_ref, scale_ref, y_ref, new_ref, z_ref = refs
    else:
        pu_ref, w_ref, scale_ref, y_ref, new_ref, z_ref = refs
    j = pl.program_id(1)
    hb = POOL_BUF + 1

    @pl.when(j == 0)
    def _():
        z_ref[0:hb, :] = jnp.zeros((hb, D_POOL), F32)
        if has_state:
            z_ref[1:hb, :] = hist_ref[...]

    @pl.when(j > 0)
    def _():
        z_ref[0:hb, :] = z_ref[tt:tt + hb, :]

    tile = pu_ref[...]
    z_ref[hb:hb + tt, :] = tile
    pos = pos0 + j * tt + lax.broadcasted_iota(jnp.int32, (tt, 1), 0)
    for g, w in enumerate(POOL_WINDOWS):
        sl = slice(g * POOL_GROUP_DIM, (g + 1) * POOL_GROUP_DIM)
        acc = tile[:, sl]
        for k in range(1, w):
            acc = acc + z_ref[hb - k:hb - k + tt, sl]
        cnt = jnp.minimum(pos + 1, w).astype(F32)
        d = acc / cnt - tile[:, sl]
        y = jnp.dot(d.astype(BF16), w_ref[g], preferred_element_type=F32)
        y_ref[:, sl] = y * scale_ref[:, sl]

    @pl.when(j == pl.num_programs(1) - 1)
    def _():
        new_ref[...] = z_ref[hb + t_last - POOL_BUF:hb + t_last, :]


def _pool_call(p3, hist, w_pool_b, pool_scale, tt, pos0, t_valid):
    b, tp, _ = p3.shape
    nt = tp // tt
    t_last = t_valid - (nt - 1) * tt
    has_state = hist is not None
    in_specs = [pl.BlockSpec((None, tt, D_POOL), lambda i, j: (i, j, 0))]
    args = [p3]
    if has_state:
        in_specs.append(pl.BlockSpec((None, POOL_BUF, D_POOL), lambda i, j: (i, 0, 0)))
        args.append(hist)
    in_specs += [pl.BlockSpec((4, POOL_GROUP_DIM, POOL_GROUP_DIM), lambda i, j: (0, 0, 0)),
                 pl.BlockSpec((1, D_POOL), lambda i, j: (0, 0))]
    args += [w_pool_b, pool_scale]
    return pl.pallas_call(
        functools.partial(_pool_body, tt=tt, pos0=pos0, t_last=t_last, has_state=has_state),
        grid=(b, nt),
        in_specs=in_specs,
        out_specs=[pl.BlockSpec((None, tt, D_POOL), lambda i, j: (i, j, 0)),
                   pl.BlockSpec((None, POOL_BUF, D_POOL), lambda i, j: (i, 0, 0))],
        out_shape=[jax.ShapeDtypeStruct((b, tp, D_POOL), F32),
                   jax.ShapeDtypeStruct((b, POOL_BUF, D_POOL), F32)],
        scratch_shapes=[pltpu.VMEM((POOL_BUF + 1 + tt, D_POOL), F32)],
        compiler_params=_params(("parallel", "arbitrary")),
        name="pool",
    )(*args)


def _log_sigmoid(x):
    return jnp.minimum(x, 0.0) - jnp.log1p(jnp.exp(-jnp.abs(x)))


def _mlstm_body(*refs, L, t_valid, has_state):
    if has_state:
        q_ref, k_ref, v_ref, g_ref, s0_ref, n0_ref, m0_ref, h_ref, s_ref, n_ref, m_ref = refs
    else:
        q_ref, k_ref, v_ref, g_ref, h_ref, s_ref, n_ref, m_ref = refs
    c = pl.program_id(1)

    @pl.when(c == 0)
    def _():
        if has_state:
            s_ref[...] = s0_ref[...]
            n_ref[...] = n0_ref[...]
            m_ref[...] = m0_ref[...]
        else:
            s_ref[...] = jnp.zeros(s_ref.shape, F32)
            n_ref[...] = jnp.zeros(n_ref.shape, F32)
            m_ref[...] = jnp.zeros(m_ref.shape, F32)

    g = g_ref[...]
    row = c * L + lax.broadcasted_iota(jnp.int32, (L, 1), 0)
    valid = row < t_valid
    lf_all = jnp.where(valid, _log_sigmoid(g), 0.0)
    ri = lax.broadcasted_iota(jnp.int32, (L, L), 0)
    ci = lax.broadcasted_iota(jnp.int32, (L, L), 1)
    causal = ri >= ci
    eye = ri == ci
    b_all = jnp.dot(causal.astype(F32), lf_all, preferred_element_type=F32,
                    precision=lax.Precision.HIGHEST)
    q = q_ref[...]
    k = k_ref[...]
    v = v_ref[...]
    for hd in range(N_HEADS):
        sl = slice(hd * HEAD_DIM, (hd + 1) * HEAD_DIM)
        ig_col = jnp.where(valid, g[:, hd:hd + 1], NEG)
        b_col = b_all[:, N_HEADS + hd:N_HEADS + hd + 1]
        c_row = jnp.sum(jnp.where(eye, ig_col - b_col, 0.0), axis=0, keepdims=True)
        d = jnp.where(causal, b_col + c_row, NEG)
        m_prev = m_ref[hd:hd + 1, 0:1]
        a_prev = b_col + m_prev
        m_t = jnp.maximum(a_prev, jnp.max(d, axis=1, keepdims=True))
        w_prev = jnp.exp(a_prev - m_t)
        pw = jnp.exp(d - m_t)
        qh = q[:, sl]
        kh = jnp.where(valid, k[:, sl] * (HEAD_DIM ** -0.5), 0.0)
        vh = jnp.where(valid, v[:, sl], 0.0)
        qb = qh.astype(BF16)
        kb = kh.astype(BF16)
        vb = vh.astype(BF16)
        qk = lax.dot_general(qb, kb, (((1,), (1,)), ((), ())), preferred_element_type=F32) * pw
        s_old = s_ref[hd]
        n_old = n_ref[hd:hd + 1, :]
        num = w_prev * jnp.dot(qb, s_old.astype(BF16), preferred_element_type=F32) \
            + jnp.dot(qk.astype(BF16), vb, preferred_element_type=F32)
        den = w_prev * jnp.sum(qh * n_old, axis=1, keepdims=True) + jnp.sum(qk, axis=1, keepdims=True)
        h_ref[:, sl] = num / jnp.maximum(jnp.abs(den), jnp.exp(-m_t))
        m_new = m_t[L - 1:L, :]
        b_last = b_col[L - 1:L, :]
        g_prev = jnp.exp(b_last + m_prev - m_new)
        g_in = jnp.exp(b_last - b_col + ig_col - m_new)
        kg = kh * g_in
        s_ref[hd] = g_prev * s_old + lax.dot_general(kg.astype(BF16), vb, (((0,), (0,)), ((), ())),
                                                     preferred_element_type=F32)
        n_ref[hd:hd + 1, :] = g_prev * n_old + jnp.sum(kg, axis=0, keepdims=True)
        m_ref[hd:hd + 1, :] = jnp.broadcast_to(m_new, (1, LANES))


def _mlstm_call(p3, state, L, t_valid):
    b, tp, _ = p3.shape
    nc = tp // L
    has_state = state is not None
    col = lambda blk: pl.BlockSpec((None, L, D_MLSTM), lambda i, j, blk=blk: (i, j, blk))
    in_specs = [col(1), col(2), col(3),
                pl.BlockSpec((None, L, LANES), lambda i, j: (i, j, D_MAIN // LANES))]
    args = [p3, p3, p3, p3]
    st_specs = [pl.BlockSpec((None, N_HEADS, HEAD_DIM, HEAD_DIM), lambda i, j: (i, 0, 0, 0)),
                pl.BlockSpec((None, N_HEADS, HEAD_DIM), lambda i, j: (i, 0, 0)),
                pl.BlockSpec((None, N_HEADS, LANES), lambda i, j: (i, 0, 0))]
    if has_state:
        in_specs += st_specs
        args += list(state)
    return pl.pallas_call(
        functools.partial(_mlstm_body, L=L, t_valid=t_valid, has_state=has_state),
        grid=(b, nc),
        in_specs=in_specs,
        out_specs=[pl.BlockSpec((None, L, D_MLSTM), lambda i, j: (i, j, 0))] + st_specs,
        out_shape=[jax.ShapeDtypeStruct((b, tp, D_MLSTM), F32),
                   jax.ShapeDtypeStruct((b, N_HEADS, HEAD_DIM, HEAD_DIM), F32),
                   jax.ShapeDtypeStruct((b, N_HEADS, HEAD_DIM), F32),
                   jax.ShapeDtypeStruct((b, N_HEADS, LANES), F32)],
        compiler_params=_params(("parallel", "arbitrary")),
        name="mlstm",
    )(*args)


def _mixout_body(yp_ref, h_ref, o_ref, x_ref, g1_ref, sh2_ref, sc2_ref, mg_ref, w_ref, lg_ref, lb_ref,
                 x1_ref, u2_ref):
    h = h_ref[...]
    o = o_ref[...]
    parts = [yp_ref[...].astype(BF16)]
    for hd in range(N_HEADS):
        sl = slice(hd * HEAD_DIM, (hd + 1) * HEAD_DIM)
        yh = _ln_rows(h[:, sl]) * mg_ref[:, sl] * jax.nn.sigmoid(o[:, sl])
        parts.append(yh.astype(BF16))
    ycat = jnp.concatenate(parts, axis=1)
    y = jnp.dot(ycat, w_ref[...], preferred_element_type=F32)
    x1 = _ln_rows(ALPHA * x_ref[...] + (1.0 + g1_ref[0]) * y) * lg_ref[...] + lb_ref[...]
    x1_ref[...] = x1
    u2_ref[...] = (_ln_rows(x1) * (1.0 + sc2_ref[0]) + sh2_ref[0]).astype(BF16)


def _mixout_call(ypool, h, p, x2d, g1, sh2, sc2, mh_g, w_out_b, ln1_g, ln1_b, tm, tiles_per_mod):
    n = x2d.shape[0]
    mr = g1.shape[1]
    row = lambda w, blk=0: pl.BlockSpec((tm, w), lambda i, blk=blk: (i, blk))
    full = lambda r, c: pl.BlockSpec((r, c), lambda i: (0, 0))
    ms = _mod_spec(mr, tiles_per_mod)
    return pl.pallas_call(
        _mixout_body,
        grid=(n // tm,),
        in_specs=[row(D_POOL), row(D_MLSTM), row(D_MLSTM, 4), row(D_MODEL), ms, ms, ms,
                  full(1, D_MLSTM), full(D_MODEL, D_MODEL), full(1, D_MODEL), full(1, D_MODEL)],
        out_specs=[row(D_MODEL), row(D_MODEL)],
        out_shape=[jax.ShapeDtypeStruct((n, D_MODEL), F32), jax.ShapeDtypeStruct((n, D_MODEL), BF16)],
        compiler_params=_params(("parallel",)),
        name="mixout",
    )(ypool, h, p, x2d, g1, sh2, sc2, mh_g, w_out_b, ln1_g, ln1_b)


N_RANK = TOPK + 1
_STAIRS = [(a, b) for a in range(N_RANK) for b in range(N_RANK) if (a + 1) * (b + 1) <= N_RANK]
N_CAND = len(_STAIRS)
N_CAND_PAD = -(-N_CAND // 8) * 8
N_RANK_PAD = -(-N_RANK // 8) * 8


def _top_values(work, n, store):
    for r in range(n):
        m = jnp.max(work, axis=0, keepdims=True)
        store(r, m)
        work = jnp.where(work == m, -jnp.inf, work)


def _route_body(u_ref, wq_ref, keys_ref, t_ref, a_ref, s1_ref, e1_ref, s_scr, sv_scr, cand_scr, ecand_scr, *, tb):
    q = jnp.dot(u_ref[...], wq_ref[...], preferred_element_type=F32)
    for hp in range(2 * PEER_HEADS):
        qhp = q[:, hp * HALF_KEY:(hp + 1) * HALF_KEY].astype(BF16)
        s_scr[hp] = lax.dot_general(keys_ref[hp], qhp, (((1,), (1,)), ((), ())),
                                    preferred_element_type=F32)

    def rank_body(hp, carry):
        def store(r, m):
            sv_scr[hp, r:r + 1, :] = m
        _top_values(s_scr[hp], N_RANK, store)
        return carry

    lax.fori_loop(0, 2 * PEER_HEADS, rank_body, 0)

    if N_CAND_PAD > N_CAND:
        cand_scr[N_CAND:N_CAND_PAD, :] = jnp.full((N_CAND_PAD - N_CAND, tb), -jnp.inf, F32)
        ecand_scr[N_CAND:N_CAND_PAD, :] = jnp.zeros((N_CAND_PAD - N_CAND, tb), F32)
    for h in range(PEER_HEADS):
        sv0 = sv_scr[2 * h]
        sv1 = sv_scr[2 * h + 1]
        ev0 = jnp.exp(sv0 - sv0[0:1])
        ev1 = jnp.exp(sv1 - sv1[0:1])
        for idx, (a, b) in enumerate(_STAIRS):
            cand_scr[idx:idx + 1, :] = sv0[a:a + 1] + sv1[b:b + 1]
            ecand_scr[idx:idx + 1, :] = ev0[a:a + 1] * ev1[b:b + 1]
        cand = cand_scr[...]
        tops = {}

        def keep(r, m):
            tops[r] = m
        _top_values(cand, N_RANK, keep)
        tau = 0.5 * (tops[TOPK - 1] + tops[TOPK])
        z = jnp.sum(jnp.where(cand > tau, ecand_scr[...], 0.0), axis=0, keepdims=True)
        s0 = s_scr[2 * h]
        s1 = s_scr[2 * h + 1]
        tmat = tau - s0
        amat = jnp.exp(s0 - sv0[0:1]) / z
        emat = jnp.exp(s1 - sv1[0:1])
        rows = slice(h * N_KEYS, (h + 1) * N_KEYS)
        for l in range(tb // LANES):
            ls = slice(l * LANES, (l + 1) * LANES)
            t_ref[l, rows, :] = tmat[:, ls]
            a_ref[l, rows, :] = amat[:, ls]
            s1_ref[l, rows, :] = s1[:, ls]
            e1_ref[l, rows, :] = emat[:, ls]


def _route_call(u2, wq_b, keys_b, tb):
    n = u2.shape[0]
    hk = PEER_HEADS * N_KEYS
    out_spec = pl.BlockSpec((tb // LANES, hk, LANES), lambda i: (i, 0, 0))
    out_sds = jax.ShapeDtypeStruct((n // LANES, hk, LANES), F32)
    return pl.pallas_call(
        functools.partial(_route_body, tb=tb),
        grid=(n // tb,),
        in_specs=[pl.BlockSpec((tb, D_MODEL), lambda i: (i, 0)),
                  pl.BlockSpec((D_MODEL, 2 * hk), lambda i: (0, 0)),
                  pl.BlockSpec((2 * PEER_HEADS, N_KEYS, HALF_KEY), lambda i: (0, 0, 0))],
        out_specs=[out_spec] * 4,
        out_shape=[out_sds] * 4,
        scratch_shapes=[pltpu.VMEM((2 * PEER_HEADS, N_KEYS, tb), F32),
                        pltpu.VMEM((2 * PEER_HEADS, N_RANK_PAD, tb), F32),
                        pltpu.VMEM((N_CAND_PAD, tb), F32),
                        pltpu.VMEM((N_CAND_PAD, tb), F32)],
        compiler_params=_params(("parallel",)),
        name="route",
    )(u2, wq_b, keys_b)


SUB = 16


def _gelu(x):
    return 0.5 * x * (1.0 + lax.erf(x * np.float32(math.sqrt(0.5))))


def _peer_rows(i0, t_ref, a_ref, s1_ref, e1_ref, act_ref, coef_ref, *, tb, n_rows):
    for ii in range(n_rows):
        i = i0 + ii
        for l in range(tb // LANES):
            ls = slice(l * LANES, (l + 1) * LANES)
            tb_h = []
            ab_h = []
            for h in range(PEER_HEADS):
                tb_h.append(jnp.broadcast_to(t_ref[l, pl.ds(h * N_KEYS + i, 1), :], (SUB, LANES)))
                ab_h.append(jnp.broadcast_to(a_ref[l, pl.ds(h * N_KEYS + i, 1), :], (SUB, LANES)))
            for jg in range(N_KEYS // SUB):
                acc = jnp.zeros((SUB, LANES), F32)
                for h in range(PEER_HEADS):
                    rs = slice(h * N_KEYS + jg * SUB, h * N_KEYS + (jg + 1) * SUB)
                    acc = acc + jnp.where(s1_ref[l, rs, :] >= tb_h[h], e1_ref[l, rs, :], 0.0) * ab_h[h]
                rows = slice(ii * N_KEYS + jg * SUB, ii * N_KEYS + (jg + 1) * SUB)
                coef_ref[rows, ls] = (acc * _gelu(act_ref[rows, ls])).astype(BF16)


def _peer_body(u_ref, t_ref, a_ref, s1_ref, e1_ref, ut0_ref, uta_ref, utb_ref, vt_ref, o_ref,
               uT_scr, act0, act1, coef0, coef1, *, tb, ek):
    c = pl.program_id(1)
    sub = ek // 2
    n_rows = sub // N_KEYS

    @pl.when(c == 0)
    def _():
        uT_scr[...] = u_ref[...].astype(F32).T.astype(BF16)
        act0[...] = jnp.dot(ut0_ref[...], uT_scr[...], preferred_element_type=F32)
        o_ref[...] = jnp.zeros(o_ref.shape, F32)

    rows = functools.partial(_peer_rows, t_ref=t_ref, a_ref=a_ref, s1_ref=s1_ref, e1_ref=e1_ref, tb=tb,
                             n_rows=n_rows)
    act1[...] = jnp.dot(uta_ref[...], uT_scr[...], preferred_element_type=F32)
    rows(c * (2 * n_rows), act_ref=act0, coef_ref=coef0)
    y0 = jnp.dot(vt_ref[:, 0:sub], coef0[...], preferred_element_type=F32)
    act0[...] = jnp.dot(utb_ref[...], uT_scr[...], preferred_element_type=F32)
    rows(c * (2 * n_rows) + n_rows, act_ref=act1, coef_ref=coef1)
    o_ref[...] += y0 + jnp.dot(vt_ref[:, sub:ek], coef1[...], preferred_element_type=F32)


def _peer_call(u2, tT, aT, s1T, e1T, u_b, vt_b, tb, ek):
    n = u2.shape[0]
    hk = PEER_HEADS * N_KEYS
    sub = ek // 2
    last = N_EXPERTS // sub - 1
    rt = pl.BlockSpec((tb // LANES, hk, LANES), lambda i, c: (i, 0, 0))
    return pl.pallas_call(
        functools.partial(_peer_body, tb=tb, ek=ek),
        grid=(n // tb, N_EXPERTS // ek),
        in_specs=[pl.BlockSpec((tb, D_MODEL), lambda i, c: (i, 0)),
                  rt, rt, rt, rt,
                  pl.BlockSpec((sub, D_MODEL), lambda i, c: (0, 0)),
                  pl.BlockSpec((sub, D_MODEL), lambda i, c: (2 * c + 1, 0)),
                  pl.BlockSpec((sub, D_MODEL), lambda i, c: (jnp.minimum(2 * c + 2, last), 0)),
                  pl.BlockSpec((D_MODEL, ek), lambda i, c: (0, c))],
        out_specs=pl.BlockSpec((D_MODEL, tb), lambda i, c: (0, i)),
        out_shape=jax.ShapeDtypeStruct((D_MODEL, n), F32),
        scratch_shapes=[pltpu.VMEM((D_MODEL, tb), BF16),
                        pltpu.VMEM((sub, tb), F32), pltpu.VMEM((sub, tb), F32),
                        pltpu.VMEM((sub, tb), BF16), pltpu.VMEM((sub, tb), BF16)],
        compiler_params=_params(("parallel", "arbitrary")),
        name="peer",
    )(u2, tT, aT, s1T, e1T, u_b, u_b, u_b, vt_b)


def _final_body(yt_ref, x1_ref, g2_ref, lg_ref, lb_ref, o_ref):
    z = ALPHA * x1_ref[...] + (1.0 + g2_ref[0]) * yt_ref[...].T
    o_ref[...] = _ln_rows(z) * lg_ref[...] + lb_ref[...]


def _final_call(yT, x1, g2, ln2_g, ln2_b, tm, tiles_per_mod):
    n = x1.shape[0]
    mr = g2.shape[1]
    full = pl.BlockSpec((1, D_MODEL), lambda i: (0, 0))
    return pl.pallas_call(
        _final_body,
        grid=(n // tm,),
        in_specs=[pl.BlockSpec((D_MODEL, tm), lambda i: (0, i)),
                  pl.BlockSpec((tm, D_MODEL), lambda i: (i, 0)),
                  _mod_spec(mr, tiles_per_mod), full, full],
        out_specs=pl.BlockSpec((tm, D_MODEL), lambda i: (i, 0)),
        out_shape=jax.ShapeDtypeStruct((n, D_MODEL), F32),
        compiler_params=_params(("parallel",)),
        name="final",
    )(yT, x1, g2, ln2_g, ln2_b)


def _layer(x, mod, pool_buf, state, pos0, wts, *, tm, tt, L, tb_route, tb_peer, ek):
    (w_in_p, b_in_p, w_pool_b, pool_scale, mh_g, w_out_b, ln1_g, ln1_b, wq_b, keys_b, u_b, vt_b,
     ln2_g, ln2_b) = wts
    b, t, _ = x.shape
    n = b * t
    x2d = x.reshape(n, D_MODEL)
    if t >= tm:
        mod3 = mod.reshape(b, 1, 6 * D_MODEL)
        tpm = lambda tile: t // tile
    else:
        assert n == tm
        mod3 = jnp.repeat(mod, t, axis=0).reshape(1, n, 6 * D_MODEL)
        tpm = lambda tile: 1
    sh1, sc1, g1, sh2, sc2, g2 = [mod3[:, :, k * D_MODEL:(k + 1) * D_MODEL] for k in range(6)]

    p = _inproj_call(x2d, sh1, sc1, w_in_p, b_in_p, tm, tpm(tm))
    p3 = p.reshape(b, t, D_IN_PAD)
    tp = -(-t // L) * L
    if tp != t:
        p3 = jnp.pad(p3, ((0, 0), (0, tp - t), (0, 0)))
    ypool, pool_new = _pool_call(p3, pool_buf, w_pool_b, pool_scale, min(tt, tp), pos0, t)
    h, s_new, n_new, m_new = _mlstm_call(p3, state, L, t)
    if tp != t:
        ypool = ypool[:, :t]
        h = h[:, :t]
    x1, u2 = _mixout_call(ypool.reshape(n, D_POOL), h.reshape(n, D_MLSTM), p, x2d, g1, sh2, sc2,
                          mh_g, w_out_b, ln1_g, ln1_b, tm, tpm(tm))
    tT, aT, s1T, e1T = _route_call(u2, wq_b, keys_b, tb_route)
    yT = _peer_call(u2, tT, aT, s1T, e1T, u_b, vt_b, tb_peer, ek)
    y = _final_call(yT, x1, g2, ln2_g, ln2_b, tm, tpm(tm))
    return y.reshape(b, t, D_MODEL), pool_new, s_new, n_new, m_new[:, :, 0]


def kernel(x_prompt, x_sample, c_prompt, c_sample, state_pool, state_C, state_n, state_m, w_mod, b_mod, w_in,
           b_in, w_pool, pool_scale, mh_norm_g, w_out, ln1_g, ln1_b, w_q, sub_keys, u_tab, v_tab, ln2_g, ln2_b):
    assert w_mod.shape[0] == DEPTH == 1
    bp = x_prompt.shape[0]
    w_in_p = jnp.pad(w_in[0], ((0, 0), (0, D_IN_PAD - D_IN))).astype(BF16)
    b_in_p = jnp.pad(b_in[0], (0, D_IN_PAD - D_IN)).reshape(1, D_IN_PAD)
    wts = (w_in_p, b_in_p, w_pool[0].astype(BF16), pool_scale[0].reshape(1, D_POOL),
           mh_norm_g[0].reshape(1, D_MLSTM), w_out[0].astype(BF16), ln1_g[0].reshape(1, D_MODEL),
           ln1_b[0].reshape(1, D_MODEL), w_q[0].astype(BF16),
           sub_keys[0].reshape(2 * PEER_HEADS, N_KEYS, HALF_KEY).astype(BF16),
           u_tab[0].astype(BF16), v_tab[0].astype(BF16).T,
           ln2_g[0].reshape(1, D_MODEL), ln2_b[0].reshape(1, D_MODEL))
    mod = _mod_call(jnp.concatenate([c_prompt, c_sample], axis=0), w_mod[0], b_mod[0])

    yp, pool_p, c_p, n_p, m_p = _layer(x_prompt, mod[:bp], None, None, 0, wts,
                                       tm=512, tt=256, L=128, tb_route=256, tb_peer=512, ek=2048)
    state = (state_C[0], state_n[0],
             jnp.broadcast_to(state_m[0][:, :, None], state_m.shape[1:] + (LANES,)))
    ys, pool_s, c_s, n_s, m_s = _layer(x_sample, mod[bp:], state_pool[0], state, PAST_LEN, wts,
                                       tm=512, tt=8, L=8, tb_route=256, tb_peer=512, ek=2048)
    return (yp, ys, pool_p[None], c_p[None], n_p[None], m_p[None],
            pool_s[None], c_s[None], n_s[None], m_s[None])
```

```python
import functools
import math

import numpy as np
import jax
import jax.numpy as jnp
from jax import lax
from jax.experimental import pallas as pl
from jax.experimental.pallas import tpu as pltpu

D_MODEL = 1024
DEPTH = 1
PAST_LEN = 16384
D_POOL = 512
POOL_WINDOWS = (2, 4, 8, 16)
POOL_GROUP_DIM = 128
POOL_BUF = 15
D_MLSTM = 512
N_HEADS = 4
HEAD_DIM = 128
D_IN = D_POOL + 4 * D_MLSTM + 2 * N_HEADS
D_MAIN = D_POOL + 4 * D_MLSTM
D_IN_PAD = D_MAIN + 128
N_KEYS = 128
N_EXPERTS = N_KEYS * N_KEYS
PEER_HEADS = 8
HALF_KEY = 128
TOPK = 16
ALPHA = (2.0 * DEPTH) ** 0.25
LN_EPS = 1e-5
NEG = -1e30

LANES = 128
VMEM_LIMIT = 56 * 1024 * 1024

F32 = jnp.float32
BF16 = jnp.bfloat16


def _ln_rows(x):
    mu = jnp.mean(x, axis=-1, keepdims=True)
    xc = x - mu
    var = jnp.mean(xc * xc, axis=-1, keepdims=True)
    return xc * lax.rsqrt(var + LN_EPS)


def _params(sem):
    return pltpu.CompilerParams(dimension_semantics=sem, vmem_limit_bytes=VMEM_LIMIT)


def _mod_body(c_ref, w_ref, b_ref, o_ref):
    c = c_ref[...]
    a = (c * jax.nn.sigmoid(c)).astype(BF16)
    o_ref[...] = jnp.dot(a, w_ref[...].astype(BF16), preferred_element_type=F32) + b_ref[...]


def _mod_call(c, w_mod, b_mod):
    nb = c.shape[0]
    tn = 1536
    return pl.pallas_call(
        _mod_body,
        grid=(6 * D_MODEL // tn,),
        in_specs=[pl.BlockSpec((nb, D_MODEL), lambda j: (0, 0)),
                  pl.BlockSpec((D_MODEL, tn), lambda j: (0, j)),
                  pl.BlockSpec((1, tn), lambda j: (0, j))],
        out_specs=pl.BlockSpec((nb, tn), lambda j: (0, j)),
        out_shape=jax.ShapeDtypeStruct((nb, 6 * D_MODEL), F32),
        compiler_params=_params(("arbitrary",)),
        name="mod",
    )(c, w_mod, b_mod.reshape(1, -1))


def _inproj_body(x_ref, sh_ref, sc_ref, w_ref, b_ref, p_ref):
    u = _ln_rows(x_ref[...]) * (1.0 + sc_ref[0]) + sh_ref[0]
    p_ref[...] = jnp.dot(u.astype(BF16), w_ref[...], preferred_element_type=F32) + b_ref[...]


def _mod_spec(mr, tiles_per_mod):
    return pl.BlockSpec((1, mr, D_MODEL), lambda i: (i // tiles_per_mod, 0, 0))


def _inproj_call(x2d, sh, sc, w_in_p, b_in_p, tm, tiles_per_mod):
    n = x2d.shape[0]
    mr = sh.shape[1]
    return pl.pallas_call(
        _inproj_body,
        grid=(n // tm,),
        in_specs=[pl.BlockSpec((tm, D_MODEL), lambda i: (i, 0)),
                  _mod_spec(mr, tiles_per_mod), _mod_spec(mr, tiles_per_mod),
                  pl.BlockSpec((D_MODEL, D_IN_PAD), lambda i: (0, 0)),
                  pl.BlockSpec((1, D_IN_PAD), lambda i: (0, 0))],
        out_specs=pl.BlockSpec((tm, D_IN_PAD), lambda i: (i, 0)),
        out_shape=jax.ShapeDtypeStruct((n, D_IN_PAD), F32),
        compiler_params=_params(("parallel",)),
        name="inproj",
    )(x2d, sh, sc, w_in_p, b_in_p)


def _pool_body(*refs, bb, tt, pos0, t_last, has_state):
    if has_state:
        pu_ref, hist_ref, w_ref, scale_ref, y_ref, new_ref, z_ref, d_ref = refs
    else:
        pu_ref, w_ref, scale_ref, y_ref, new_ref, z_ref, d_ref = refs
    j = pl.program_id(1)
    hb = POOL_BUF + 1
    pos = pos0 + j * tt + lax.broadcasted_iota(jnp.int32, (tt, 1), 0)
    for bi in range(bb):
        @pl.when(j == 0)
        def _():
            z_ref[bi, 0:hb, :] = jnp.zeros((hb, D_POOL), F32)
            if has_state:
                z_ref[bi, 1:hb, :] = hist_ref[bi]

        @pl.when(j > 0)
        def _():
            z_ref[bi, 0:hb, :] = z_ref[bi, tt:tt + hb, :]

        tile = pu_ref[bi]
        z_ref[bi, hb:hb + tt, :] = tile
        for g, w in enumerate(POOL_WINDOWS):
            sl = slice(g * POOL_GROUP_DIM, (g + 1) * POOL_GROUP_DIM)
            acc = tile[:, sl]
            for k in range(1, w):
                acc = acc + z_ref[bi, hb - k:hb - k + tt, sl]
            cnt = jnp.minimum(pos + 1, w).astype(F32)
            d_ref[bi * tt:(bi + 1) * tt, sl] = acc / cnt - tile[:, sl]
    for g in range(len(POOL_WINDOWS)):
        sl = slice(g * POOL_GROUP_DIM, (g + 1) * POOL_GROUP_DIM)
        y = jnp.dot(d_ref[:, sl].astype(BF16), w_ref[g], preferred_element_type=F32) * scale_ref[:, sl]
        for bi in range(bb):
            y_ref[bi, :, sl] = y[bi * tt:(bi + 1) * tt]

    @pl.when(j == pl.num_programs(1) - 1)
    def _():
        for bi in range(bb):
            new_ref[bi] = z_ref[bi, hb + t_last - POOL_BUF:hb + t_last, :]


def _pool_call(p3, hist, w_pool_b, pool_scale, bb, tt, pos0, t_valid):
    b, tp, _ = p3.shape
    nt = tp // tt
    assert bb == 1 or nt == 1
    t_last = t_valid - (nt - 1) * tt
    has_state = hist is not None
    in_specs = [pl.BlockSpec((bb, tt, D_POOL), lambda i, j: (i, j, 0))]
    args = [p3]
    if has_state:
        in_specs.append(pl.BlockSpec((bb, POOL_BUF, D_POOL), lambda i, j: (i, 0, 0)))
        args.append(hist)
    in_specs += [pl.BlockSpec((4, POOL_GROUP_DIM, POOL_GROUP_DIM), lambda i, j: (0, 0, 0)),
                 pl.BlockSpec((1, D_POOL), lambda i, j: (0, 0))]
    args += [w_pool_b, pool_scale]
    return pl.pallas_call(
        functools.partial(_pool_body, bb=bb, tt=tt, pos0=pos0, t_last=t_last, has_state=has_state),
        grid=(b // bb, nt),
        in_specs=in_specs,
        out_specs=[pl.BlockSpec((bb, tt, D_POOL), lambda i, j: (i, j, 0)),
                   pl.BlockSpec((bb, POOL_BUF, D_POOL), lambda i, j: (i, 0, 0))],
        out_shape=[jax.ShapeDtypeStruct((b, tp, D_POOL), F32),
                   jax.ShapeDtypeStruct((b, POOL_BUF, D_POOL), F32)],
        scratch_shapes=[pltpu.VMEM((bb, POOL_BUF + 1 + tt, D_POOL), F32),
                        pltpu.VMEM((bb * tt, D_POOL), F32)],
        compiler_params=_params(("parallel", "arbitrary")),
        name="pool",
    )(*args)


def _log_sigmoid(x):
    return jnp.minimum(x, 0.0) - jnp.log1p(jnp.exp(-jnp.abs(x)))


def _mlstm_body(*refs, bb, L, t_valid, has_state):
    if has_state:
        q_ref, k_ref, v_ref, g_ref, s0_ref, n0_ref, m0_ref, h_ref, s_ref, n_ref, m_ref = refs
    else:
        q_ref, k_ref, v_ref, g_ref, h_ref, s_ref, n_ref, m_ref = refs
    c = pl.program_id(1)

    @pl.when(c == 0)
    def _():
        if has_state:
            s_ref[...] = s0_ref[...]
            n_ref[...] = n0_ref[...]
            m_ref[...] = m0_ref[...]
        else:
            s_ref[...] = jnp.zeros(s_ref.shape, F32)
            n_ref[...] = jnp.zeros(n_ref.shape, F32)
            m_ref[...] = jnp.zeros(m_ref.shape, F32)

    def one(bi, carry):
        _mlstm_chunk(c, q_ref.at[bi], k_ref.at[bi], v_ref.at[bi], g_ref.at[bi], h_ref.at[bi],
                     s_ref.at[bi], n_ref.at[bi], m_ref.at[bi], L=L, t_valid=t_valid)
        return carry

    if bb == 1:
        one(0, 0)
    else:
        lax.fori_loop(0, bb, one, 0, unroll=2)


def _mlstm_chunk(c, q_ref, k_ref, v_ref, g_ref, h_ref, s_ref, n_ref, m_ref, *, L, t_valid):
    g = g_ref[...]
    row = c * L + lax.broadcasted_iota(jnp.int32, (L, 1), 0)
    valid = row < t_valid
    lf_all = jnp.where(valid, _log_sigmoid(g), 0.0)
    ri = lax.broadcasted_iota(jnp.int32, (L, L), 0)
    ci = lax.broadcasted_iota(jnp.int32, (L, L), 1)
    causal = ri >= ci
    eye = ri == ci
    b_all = jnp.dot(causal.astype(F32), lf_all, preferred_element_type=F32,
                    precision=lax.Precision.HIGHEST)
    q = q_ref[...]
    k = k_ref[...]
    v = v_ref[...]
    for hd in range(N_HEADS):
        sl = slice(hd * HEAD_DIM, (hd + 1) * HEAD_DIM)
        ig_col = jnp.where(valid, g[:, hd:hd + 1], NEG)
        b_col = b_all[:, N_HEADS + hd:N_HEADS + hd + 1]
        c_row = jnp.sum(jnp.where(eye, ig_col - b_col, 0.0), axis=0, keepdims=True)
        d = jnp.where(causal, b_col + c_row, NEG)
        m_prev = m_ref[hd:hd + 1, 0:1]
        a_prev = b_col + m_prev
        m_t = jnp.maximum(a_prev, jnp.max(d, axis=1, keepdims=True))
        w_prev = jnp.exp(a_prev - m_t)
        pw = jnp.exp(d - m_t)
        qh = q[:, sl]
        kh = jnp.where(valid, k[:, sl] * (HEAD_DIM ** -0.5), 0.0)
        vh = jnp.where(valid, v[:, sl], 0.0)
        qb = qh.astype(BF16)
        kb = kh.astype(BF16)
        vb = vh.astype(BF16)
        qk = lax.dot_general(qb, kb, (((1,), (1,)), ((), ())), preferred_element_type=F32) * pw
        s_old = s_ref[hd]
        n_old = n_ref[hd:hd + 1, :]
        num = w_prev * jnp.dot(qb, s_old.astype(BF16), preferred_element_type=F32) \
            + jnp.dot(qk.astype(BF16), vb, preferred_element_type=F32)
        den = w_prev * jnp.sum(qh * n_old, axis=1, keepdims=True) + jnp.sum(qk, axis=1, keepdims=True)
        h_ref[:, sl] = num / jnp.maximum(jnp.abs(den), jnp.exp(-m_t))
        m_new = m_t[L - 1:L, :]
        b_last = b_col[L - 1:L, :]
        g_prev = jnp.exp(b_last + m_prev - m_new)
        g_in = jnp.exp(b_last - b_col + ig_col - m_new)
        kg = kh * g_in
        s_ref[hd] = g_prev * s_old + lax.dot_general(kg.astype(BF16), vb, (((0,), (0,)), ((), ())),
                                                     preferred_element_type=F32)
        n_ref[hd:hd + 1, :] = g_prev * n_old + jnp.sum(kg, axis=0, keepdims=True)
        m_ref[hd:hd + 1, :] = jnp.broadcast_to(m_new, (1, LANES))


def _mlstm_call(p3, state, bb, L, t_valid):
    b, tp, _ = p3.shape
    nc = tp // L
    has_state = state is not None
    col = lambda blk: pl.BlockSpec((bb, L, D_MLSTM), lambda i, j, blk=blk: (i, j, blk))
    in_specs = [col(1), col(2), col(3),
                pl.BlockSpec((bb, L, LANES), lambda i, j: (i, j, D_MAIN // LANES))]
    args = [p3, p3, p3, p3]
    st_specs = [pl.BlockSpec((bb, N_HEADS, HEAD_DIM, HEAD_DIM), lambda i, j: (i, 0, 0, 0)),
                pl.BlockSpec((bb, N_HEADS, HEAD_DIM), lambda i, j: (i, 0, 0)),
                pl.BlockSpec((bb, N_HEADS, LANES), lambda i, j: (i, 0, 0))]
    if has_state:
        in_specs += st_specs
        args += list(state)
    return pl.pallas_call(
        functools.partial(_mlstm_body, bb=bb, L=L, t_valid=t_valid, has_state=has_state),
        grid=(b // bb, nc),
        in_specs=in_specs,
        out_specs=[pl.BlockSpec((bb, L, D_MLSTM), lambda i, j: (i, j, 0))] + st_specs,
        out_shape=[jax.ShapeDtypeStruct((b, tp, D_MLSTM), F32),
                   jax.ShapeDtypeStruct((b, N_HEADS, HEAD_DIM, HEAD_DIM), F32),
                   jax.ShapeDtypeStruct((b, N_HEADS, HEAD_DIM), F32),
                   jax.ShapeDtypeStruct((b, N_HEADS, LANES), F32)],
        compiler_params=_params(("parallel", "arbitrary")),
        name="mlstm",
    )(*args)


def _mixout_body(yp_ref, h_ref, o_ref, x_ref, g1_ref, sh2_ref, sc2_ref, mg_ref, w_ref, lg_ref, lb_ref,
                 x1_ref, u2_ref):
    h = h_ref[...]
    o = o_ref[...]
    parts = [yp_ref[...].astype(BF16)]
    for hd in range(N_HEADS):
        sl = slice(hd * HEAD_DIM, (hd + 1) * HEAD_DIM)
        yh = _ln_rows(h[:, sl]) * mg_ref[:, sl] * jax.nn.sigmoid(o[:, sl])
        parts.append(yh.astype(BF16))
    ycat = jnp.concatenate(parts, axis=1)
    y = jnp.dot(ycat, w_ref[...], preferred_element_type=F32)
    x1 = _ln_rows(ALPHA * x_ref[...] + (1.0 + g1_ref[0]) * y) * lg_ref[...] + lb_ref[...]
    x1_ref[...] = x1
    u2_ref[...] = (_ln_rows(x1) * (1.0 + sc2_ref[0]) + sh2_ref[0]).astype(BF16)


def _mixout_call(ypool, h, p, x2d, g1, sh2, sc2, mh_g, w_out_b, ln1_g, ln1_b, tm, tiles_per_mod):
    n = x2d.shape[0]
    mr = g1.shape[1]
    row = lambda w, blk=0: pl.BlockSpec((tm, w), lambda i, blk=blk: (i, blk))
    full = lambda r, c: pl.BlockSpec((r, c), lambda i: (0, 0))
    ms = _mod_spec(mr, tiles_per_mod)
    return pl.pallas_call(
        _mixout_body,
        grid=(n // tm,),
        in_specs=[row(D_POOL), row(D_MLSTM), row(D_MLSTM, 4), row(D_MODEL), ms, ms, ms,
                  full(1, D_MLSTM), full(D_MODEL, D_MODEL), full(1, D_MODEL), full(1, D_MODEL)],
        out_specs=[row(D_MODEL), row(D_MODEL)],
        out_shape=[jax.ShapeDtypeStruct((n, D_MODEL), F32), jax.ShapeDtypeStruct((n, D_MODEL), BF16)],
        compiler_params=_params(("parallel",)),
        name="mixout",
    )(ypool, h, p, x2d, g1, sh2, sc2, mh_g, w_out_b, ln1_g, ln1_b)


_STAIRS = [(a, b) for a in range(TOPK) for b in range(TOPK) if (a + 1) * (b + 1) <= TOPK]
_STAIR_ROWS = [(min(k for k, ab in enumerate(_STAIRS) if ab[0] == a),
                1 + max(k for k, ab in enumerate(_STAIRS) if ab[0] == a)) for a in range(TOPK)]
N_CAND = len(_STAIRS)
N_CAND_PAD = -(-N_CAND // 8) * 8


def _top_values(work, n, store, rank=None):
    for r in range(n):
        m = jnp.max(work, axis=0, keepdims=True)
        store(r, m)
        hit = work == m
        if rank is not None:
            rank = jnp.where(hit, np.float32(r), rank)
        work = jnp.where(hit, -jnp.inf, work)
    return rank


def _route_body(u_ref, wq_ref, keys_ref, r1_ref, e1_ref, c_ref, a_ref, s_scr, sv_scr, rk_scr, cand_scr, ecand_scr,
                *, tb):
    q = jnp.dot(u_ref[...], wq_ref[...], preferred_element_type=F32)
    for hp in range(2 * PEER_HEADS):
        qhp = q[:, hp * HALF_KEY:(hp + 1) * HALF_KEY].astype(BF16)
        s_scr[hp] = lax.dot_general(keys_ref[hp], qhp, (((1,), (1,)), ((), ())),
                                    preferred_element_type=F32)

    def rank_body(h, carry):
        for p in range(2):
            hp = 2 * h + p
            for l in range(tb // LANES):
                ls = slice(l * LANES, (l + 1) * LANES)

                def store(r, m, hp=hp, ls=ls):
                    sv_scr[hp, r:r + 1, ls] = m
                work = s_scr[hp, :, ls]
                if p == 0:
                    _top_values(work, TOPK, store)
                else:
                    rk_scr[h, :, ls] = _top_values(work, TOPK, store, jnp.full(work.shape, TOPK, F32))
        return carry

    lax.fori_loop(0, PEER_HEADS, rank_body, 0)

    if N_CAND_PAD > N_CAND:
        cand_scr[N_CAND:N_CAND_PAD, :] = jnp.full((N_CAND_PAD - N_CAND, tb), -jnp.inf, F32)
        ecand_scr[N_CAND:N_CAND_PAD, :] = jnp.zeros((N_CAND_PAD - N_CAND, tb), F32)
    for h in range(PEER_HEADS):
        sv0 = sv_scr[2 * h]
        sv1 = sv_scr[2 * h + 1]
        ev0 = jnp.exp(sv0 - sv0[0:1])
        ev1 = jnp.exp(sv1 - sv1[0:1])
        for idx, (a, b) in enumerate(_STAIRS):
            cand_scr[idx:idx + 1, :] = sv0[a:a + 1] + sv1[b:b + 1]
            ecand_scr[idx:idx + 1, :] = ev0[a:a + 1] * ev1[b:b + 1]
        tops = {}

        def keep(r, m):
            tops[r] = m
        _top_values(cand_scr[...], TOPK, keep)
        c16 = tops[TOPK - 1]
        z = jnp.sum(jnp.where(cand_scr[...] >= c16, ecand_scr[...], 0.0), axis=0, keepdims=True)
        s0 = s_scr[2 * h]
        s1 = s_scr[2 * h + 1]
        cmat = jnp.zeros(s0.shape, F32)
        for a, (lo, hi) in enumerate(_STAIR_ROWS):
            cnt_a = jnp.sum(jnp.where(cand_scr[lo:hi, :] >= c16, 1.0, 0.0), axis=0, keepdims=True)
            cmat = jnp.where(s0 == sv0[a:a + 1], cnt_a, cmat)
        amat = jnp.exp(s0 - sv0[0:1]) / z
        emat = jnp.exp(s1 - sv1[0:1])
        rmat = rk_scr[h]
        rows = slice(h * N_KEYS, (h + 1) * N_KEYS)
        prow = slice(h * N_KEYS // 2, (h + 1) * N_KEYS // 2)
        for l in range(tb // LANES):
            ls = slice(l * LANES, (l + 1) * LANES)
            r1_ref[l, prow, :] = pltpu.bitcast(rmat[:, ls].astype(BF16), jnp.uint32)
            e1_ref[l, prow, :] = pltpu.bitcast(emat[:, ls].astype(BF16), jnp.uint32)
            c_ref[l, rows, :] = cmat[:, ls]
            a_ref[l, rows, :] = amat[:, ls]


def _route_call(u2, wq_b, keys_b, tb):
    n = u2.shape[0]
    hk = PEER_HEADS * N_KEYS
    spec = lambda rows: pl.BlockSpec((tb // LANES, rows, LANES), lambda i: (i, 0, 0))
    sds = lambda rows, dt: jax.ShapeDtypeStruct((n // LANES, rows, LANES), dt)
    return pl.pallas_call(
        functools.partial(_route_body, tb=tb),
        grid=(n // tb,),
        in_specs=[pl.BlockSpec((tb, D_MODEL), lambda i: (i, 0)),
                  pl.BlockSpec((D_MODEL, 2 * hk), lambda i: (0, 0)),
                  pl.BlockSpec((2 * PEER_HEADS, N_KEYS, HALF_KEY), lambda i: (0, 0, 0))],
        out_specs=[spec(hk // 2), spec(hk // 2), spec(hk), spec(hk)],
        out_shape=[sds(hk // 2, jnp.uint32), sds(hk // 2, jnp.uint32), sds(hk, F32), sds(hk, F32)],
        scratch_shapes=[pltpu.VMEM((2 * PEER_HEADS, N_KEYS, tb), F32),
                        pltpu.VMEM((2 * PEER_HEADS, TOPK, tb), F32),
                        pltpu.VMEM((PEER_HEADS, N_KEYS, tb), F32),
                        pltpu.VMEM((N_CAND_PAD, tb), F32),
                        pltpu.VMEM((N_CAND_PAD, tb), F32)],
        compiler_params=_params(("parallel",)),
        name="route",
    )(u2, wq_b, keys_b)


SUB = 16


def _gelu(x):
    return 0.5 * x * (1.0 + lax.erf(x * np.float32(math.sqrt(0.5))))


def _peer_rows(i0, r1_ref, e1_ref, c_ref, a_ref, act_ref, coef_ref, *, tb, n_rows):
    zero = jnp.zeros((SUB, LANES), BF16)
    for ii in range(n_rows):
        i = i0 + ii
        for l in range(tb // LANES):
            ls = slice(l * LANES, (l + 1) * LANES)
            cb_h = []
            ab_h = []
            for h in range(PEER_HEADS):
                cb_h.append(jnp.broadcast_to(c_ref[l, pl.ds(h * N_KEYS + i, 1), :], (SUB, LANES)).astype(BF16))
                ab_h.append(jnp.broadcast_to(a_ref[l, pl.ds(h * N_KEYS + i, 1), :], (SUB, LANES)).astype(BF16))
            for jg in range(N_KEYS // SUB):
                acc = zero
                for h in range(PEER_HEADS):
                    rs = slice((h * N_KEYS + jg * SUB) // 2, (h * N_KEYS + (jg + 1) * SUB) // 2)
                    r1 = pltpu.bitcast(r1_ref[l, rs, :], BF16)
                    e1 = pltpu.bitcast(e1_ref[l, rs, :], BF16)
                    acc = acc + jnp.where(r1 < cb_h[h], e1, zero) * ab_h[h]
                rows = slice(ii * N_KEYS + jg * SUB, ii * N_KEYS + (jg + 1) * SUB)
                coef_ref[rows, ls] = acc * _gelu(act_ref[rows, ls]).astype(BF16)


def _peer_body(u_ref, r1_ref, e1_ref, c_ref, a_ref, ut0_ref, uta_ref, utb_ref, vt_ref, o_ref,
               uT_scr, act0, act1, coef0, coef1, *, tb, ek):
    c = pl.program_id(1)
    sub = ek // 2
    n_rows = sub // N_KEYS

    @pl.when(c == 0)
    def _():
        uT_scr[...] = u_ref[...].astype(F32).T.astype(BF16)
        act0[...] = jnp.dot(ut0_ref[...], uT_scr[...], preferred_element_type=F32)
        o_ref[...] = jnp.zeros(o_ref.shape, F32)

    rows = functools.partial(_peer_rows, r1_ref=r1_ref, e1_ref=e1_ref, c_ref=c_ref, a_ref=a_ref, tb=tb,
                             n_rows=n_rows)
    act1[...] = jnp.dot(uta_ref[...], uT_scr[...], preferred_element_type=F32)
    rows(c * (2 * n_rows), act_ref=act0, coef_ref=coef0)
    y0 = jnp.dot(vt_ref[:, 0:sub], coef0[...], preferred_element_type=F32)
    act0[...] = jnp.dot(utb_ref[...], uT_scr[...], preferred_element_type=F32)
    rows(c * (2 * n_rows) + n_rows, act_ref=act1, coef_ref=coef1)
    o_ref[...] += y0 + jnp.dot(vt_ref[:, sub:ek], coef1[...], preferred_element_type=F32)


def _peer_call(u2, r1T, e1T, cT, aT, u_b, vt_b, tb, ek):
    n = u2.shape[0]
    hk = PEER_HEADS * N_KEYS
    sub = ek // 2
    last = N_EXPERTS // sub - 1
    rt = lambda rows: pl.BlockSpec((tb // LANES, rows, LANES), lambda i, c: (i, 0, 0))
    return pl.pallas_call(
        functools.partial(_peer_body, tb=tb, ek=ek),
        grid=(n // tb, N_EXPERTS // ek),
        in_specs=[pl.BlockSpec((tb, D_MODEL), lambda i, c: (i, 0)),
                  rt(hk // 2), rt(hk // 2), rt(hk), rt(hk),
                  pl.BlockSpec((sub, D_MODEL), lambda i, c: (0, 0)),
                  pl.BlockSpec((sub, D_MODEL), lambda i, c: (2 * c + 1, 0)),
                  pl.BlockSpec((sub, D_MODEL), lambda i, c: (jnp.minimum(2 * c + 2, last), 0)),
                  pl.BlockSpec((D_MODEL, ek), lambda i, c: (0, c))],
        out_specs=pl.BlockSpec((D_MODEL, tb), lambda i, c: (0, i)),
        out_shape=jax.ShapeDtypeStruct((D_MODEL, n), F32),
        scratch_shapes=[pltpu.VMEM((D_MODEL, tb), BF16),
                        pltpu.VMEM((sub, tb), F32), pltpu.VMEM((sub, tb), F32),
                        pltpu.VMEM((sub, tb), BF16), pltpu.VMEM((sub, tb), BF16)],
        compiler_params=_params(("parallel", "arbitrary")),
        name="peer",
    )(u2, r1T, e1T, cT, aT, u_b, u_b, u_b, vt_b)


def _final_body(yt_ref, x1_ref, g2_ref, lg_ref, lb_ref, o_ref):
    z = ALPHA * x1_ref[...] + (1.0 + g2_ref[0]) * yt_ref[...].T
    o_ref[...] = _ln_rows(z) * lg_ref[...] + lb_ref[...]


def _final_call(yT, x1, g2, ln2_g, ln2_b, tm, tiles_per_mod):
    n = x1.shape[0]
    mr = g2.shape[1]
    full = pl.BlockSpec((1, D_MODEL), lambda i: (0, 0))
    return pl.pallas_call(
        _final_body,
        grid=(n // tm,),
        in_specs=[pl.BlockSpec((D_MODEL, tm), lambda i: (0, i)),
                  pl.BlockSpec((tm, D_MODEL), lambda i: (i, 0)),
                  _mod_spec(mr, tiles_per_mod), full, full],
        out_specs=pl.BlockSpec((tm, D_MODEL), lambda i: (i, 0)),
        out_shape=jax.ShapeDtypeStruct((n, D_MODEL), F32),
        compiler_params=_params(("parallel",)),
        name="final",
    )(yT, x1, g2, ln2_g, ln2_b)


def _layer(x, mod, pool_buf, state, pos0, wts, *, tm, tt, L, bb, tb_route, tb_peer, ek):
    (w_in_p, b_in_p, w_pool_b, pool_scale, mh_g, w_out_b, ln1_g, ln1_b, wq_b, keys_b, u_b, vt_b,
     ln2_g, ln2_b) = wts
    b, t, _ = x.shape
    n = b * t
    x2d = x.reshape(n, D_MODEL)
    if t >= tm:
        mod3 = mod.reshape(b, 1, 6 * D_MODEL)
        tpm = lambda tile: t // tile
    else:
        assert n == tm
        mod3 = jnp.repeat(mod, t, axis=0).reshape(1, n, 6 * D_MODEL)
        tpm = lambda tile: 1
    sh1, sc1, g1, sh2, sc2, g2 = [mod3[:, :, k * D_MODEL:(k + 1) * D_MODEL] for k in range(6)]

    p = _inproj_call(x2d, sh1, sc1, w_in_p, b_in_p, tm, tpm(tm))
    p3 = p.reshape(b, t, D_IN_PAD)
    tp = -(-t // L) * L
    if tp != t:
        p3 = jnp.pad(p3, ((0, 0), (0, tp - t), (0, 0)))
    ypool, pool_new = _pool_call(p3, pool_buf, w_pool_b, pool_scale, bb, min(tt, tp), pos0, t)
    h, s_new, n_new, m_new = _mlstm_call(p3, state, bb, L, t)
    if tp != t:
        ypool = ypool[:, :t]
        h = h[:, :t]
    x1, u2 = _mixout_call(ypool.reshape(n, D_POOL), h.reshape(n, D_MLSTM), p, x2d, g1, sh2, sc2,
                          mh_g, w_out_b, ln1_g, ln1_b, tm, tpm(tm))
    r1T, e1T, cT, aT = _route_call(u2, wq_b, keys_b, tb_route)
    yT = _peer_call(u2, r1T, e1T, cT, aT, u_b, vt_b, tb_peer, ek)
    y = _final_call(yT, x1, g2, ln2_g, ln2_b, tm, tpm(tm))
    return y.reshape(b, t, D_MODEL), pool_new, s_new, n_new, m_new[:, :, 0]


def kernel(x_prompt, x_sample, c_prompt, c_sample, state_pool, state_C, state_n, state_m, w_mod, b_mod, w_in,
           b_in, w_pool, pool_scale, mh_norm_g, w_out, ln1_g, ln1_b, w_q, sub_keys, u_tab, v_tab, ln2_g, ln2_b):
    assert w_mod.shape[0] == DEPTH == 1
    bp = x_prompt.shape[0]
    w_in_p = jnp.pad(w_in[0], ((0, 0), (0, D_IN_PAD - D_IN))).astype(BF16)
    b_in_p = jnp.pad(b_in[0], (0, D_IN_PAD - D_IN)).reshape(1, D_IN_PAD)
    wts = (w_in_p, b_in_p, w_pool[0].astype(BF16), pool_scale[0].reshape(1, D_POOL),
           mh_norm_g[0].reshape(1, D_MLSTM), w_out[0].astype(BF16), ln1_g[0].reshape(1, D_MODEL),
           ln1_b[0].reshape(1, D_MODEL), w_q[0].astype(BF16),
           sub_keys[0].reshape(2 * PEER_HEADS, N_KEYS, HALF_KEY).astype(BF16),
           u_tab[0].astype(BF16), v_tab[0].astype(BF16).T,
           ln2_g[0].reshape(1, D_MODEL), ln2_b[0].reshape(1, D_MODEL))
    mod = _mod_call(jnp.concatenate([c_prompt, c_sample], axis=0), w_mod[0], b_mod[0])

    yp, pool_p, c_p, n_p, m_p = _layer(x_prompt, mod[:bp], None, None, 0, wts,
                                       tm=512, tt=256, L=128, bb=1, tb_route=256, tb_peer=512, ek=2048)
    state = (state_C[0], state_n[0],
             jnp.broadcast_to(state_m[0][:, :, None], state_m.shape[1:] + (LANES,)))
    ys, pool_s, c_s, n_s, m_s = _layer(x_sample, mod[bp:], state_pool[0], state, PAST_LEN, wts,
                                       tm=512, tt=8, L=8, bb=8, tb_route=256, tb_peer=512, ek=2048)
    return (yp, ys, pool_p[None], c_p[None], n_p[None], m_p[None],
            pool_s[None], c_s[None], n_s[None], m_s[None])
```
